```python
import jax, jax.numpy as jnp
from jax import lax
import numpy as np

D_MODEL = 1024
BATCH = 16
SEQ = 2048
DEPTH = 1

SB_HEADS = 8
SB_HEAD_DIM = 64
SB_WIDTH = SB_HEADS * SB_HEAD_DIM
SB_BLOCK = 128
ML_HEADS = 4
ML_HEAD_DIM = 128
ML_WIDTH = ML_HEADS * ML_HEAD_DIM
ML_CHUNK = 64
CONV_WIDTH = 4
PEER_HEADS = 8
PEER_N_KEYS = 128
PEER_N_EXPERTS = PEER_N_KEYS * PEER_N_KEYS
PEER_QUERY_DIM = 256
PEER_HALF = PEER_QUERY_DIM // 2
PEER_TOPK = 16
PEER_TOKEN_BLOCK = 128

RMS_EPS = 1e-6

O_SB = 3 * SB_WIDTH
O_MLQK = O_SB + 2 * ML_WIDTH
O_MLV = O_MLQK + ML_WIDTH
O_MLO = O_MLV + ML_WIDTH
O_MLIF = O_MLO + 2 * ML_HEADS
IN_COLS = O_MLIF + 2 * D_MODEL

kernel_name = 'hybrid_sba_mlstm_peer'


def rmsnorm(x, w):
    xf = x.astype(jnp.float32)
    y = xf * lax.rsqrt(jnp.mean(xf * xf, axis=-1, keepdims=True) + RMS_EPS) * w.astype(jnp.float32)
    return y.astype(x.dtype)


def causal_short_conv(x, w, b):
    K = w.shape[0]
    S = x.shape[1]
    xp = jnp.pad(x, ((0, 0), (K - 1, 0), (0, 0)))
    y = xp[:, 0:S] * w[0]
    for j in range(1, K):
        y = y + xp[:, j:j + S] * w[j]
    return y + b


def stick_breaking_attention(q, k, v):
    S = q.shape[2]
    scale = SB_HEAD_DIM ** -0.5
    outs = []
    for blk in range(S // SB_BLOCK):
        start = blk * SB_BLOCK
        end = start + SB_BLOCK
        qb = q[:, :, start:end]
        kb = k[:, :, :end]
        vb = v[:, :, :end]
        z = jnp.einsum('bhtd,bhsd->bhts', qb, kb) * scale
        t_pos = start + jnp.arange(SB_BLOCK)[:, None]
        s_pos = jnp.arange(end)[None, :]
        causal = s_pos < t_pos
        sp = jnp.where(causal, jax.nn.softplus(z), 0.0)
        sp_next = jnp.concatenate([sp[..., 1:], jnp.zeros_like(sp[..., :1])], axis=-1)
        suffix = lax.cumsum(sp_next, axis=3, reverse=True)
        w = jnp.where(causal, jnp.exp(jax.nn.log_sigmoid(z) - suffix), 0.0)
        outs.append(jnp.einsum('bhts,bhsd->bhtd', w, vb))
    return jnp.concatenate(outs, axis=2)


def _to_chunks(a):
    B, H, S = a.shape[:3]
    a = a.reshape((B, H, S // ML_CHUNK, ML_CHUNK) + a.shape[3:])
    return jnp.moveaxis(a, 2, 0)


def mlstm_chunkwise(q, k, v, i_pre, f_pre):
    B, H, S, D = q.shape
    logf = jax.nn.log_sigmoid(f_pre)
    tri = jnp.tril(jnp.ones((ML_CHUNK, ML_CHUNK), dtype=bool))

    def step(carry, inp):
        C, n, m = carry
        qc, kc, vc, ic, lfc = inp
        b = jnp.cumsum(lfc, axis=-1)
        dmat = jnp.where(tri, b[..., :, None] - b[..., None, :] + ic[..., None, :], -jnp.inf)
        inter = b + m[..., None]
        m_t = jnp.maximum(inter, jnp.max(dmat, axis=-1))
        w_intra = jnp.exp(dmat - m_t[..., None])
        w_inter = jnp.exp(inter - m_t)
        s = jnp.einsum('bhtd,bhsd->bhts', qc, kc) * w_intra
        num = jnp.einsum('bhts,bhsd->bhtd', s, vc) + w_inter[..., None] * jnp.einsum('bhtd,bhde->bhte', qc, C)
        den = jnp.sum(s, axis=-1) + w_inter * jnp.einsum('bhtd,bhd->bht', qc, n)
        h = num / jnp.maximum(jnp.abs(den), jnp.exp(-m_t))[..., None]
        b_last = b[..., -1]
        g = b_last[..., None] - b + ic
        m_new = jnp.maximum(b_last + m, jnp.max(g, axis=-1))
        decay = jnp.exp(b_last + m - m_new)
        wk = jnp.exp(g - m_new[..., None])
        C_new = decay[..., None, None] * C + jnp.einsum('bhs,bhsd,bhse->bhde', wk, kc, vc)
        n_new = decay[..., None] * n + jnp.einsum('bhs,bhsd->bhd', wk, kc)
        return (C_new, n_new, m_new), h

    init = (jnp.zeros((B, H, D, D), jnp.float32), jnp.zeros((B, H, D), jnp.float32), jnp.zeros((B, H), jnp.float32))
    xs = (_to_chunks(q), _to_chunks(k), _to_chunks(v), _to_chunks(i_pre), _to_chunks(logf))
    _, hs = lax.scan(step, init, xs)
    return jnp.moveaxis(hs, 0, 2).reshape(B, H, S, D)


def hybrid_mixer(xn, w_in, conv_w, conv_b, b_igate, b_fgate, mlstm_norm_w, w_branch_sb, w_branch_ml, w_out):
    B, S, _ = xn.shape
    proj = xn @ w_in
    sb_qkv, ml_qk, ml_v, ml_o, ml_if, gates = jnp.split(proj, [O_SB, O_MLQK, O_MLV, O_MLO, O_MLIF], axis=-1)

    sb = sb_qkv.astype(jnp.float32).reshape(B, S, 3, SB_HEADS, SB_HEAD_DIM).transpose(2, 0, 3, 1, 4)
    y_sb = stick_breaking_attention(sb[0], sb[1], sb[2])
    y_sb = y_sb.transpose(0, 2, 1, 3).reshape(B, S, SB_WIDTH)

    qk = jax.nn.silu(causal_short_conv(ml_qk, conv_w, conv_b)).astype(jnp.float32)
    q_m, k_m = jnp.split(qk, 2, axis=-1)
    to_heads = lambda a: a.reshape(B, S, ML_HEADS, ML_HEAD_DIM).transpose(0, 2, 1, 3)
    q_m = to_heads(q_m)
    k_m = to_heads(k_m) * (ML_HEAD_DIM ** -0.5)
    v_m = to_heads(ml_v.astype(jnp.float32))
    ml_if = ml_if.astype(jnp.float32)
    i_pre = (ml_if[..., :ML_HEADS] + b_igate.astype(jnp.float32)).transpose(0, 2, 1)
    f_pre = (ml_if[..., ML_HEADS:] + b_fgate.astype(jnp.float32)).transpose(0, 2, 1)
    h = mlstm_chunkwise(q_m, k_m, v_m, i_pre, f_pre)
    h = h * lax.rsqrt(jnp.mean(h * h, axis=-1, keepdims=True) + RMS_EPS) * mlstm_norm_w.astype(jnp.float32).reshape(ML_HEADS, 1, ML_HEAD_DIM)
    y_ml = jax.nn.sigmoid(ml_o.astype(jnp.float32)) * h.transpose(0, 2, 1, 3).reshape(B, S, ML_WIDTH)

    g_sb, g_ml = jnp.split(gates.astype(jnp.float32), 2, axis=-1)
    merged = jax.nn.sigmoid(g_sb) * (y_sb @ w_branch_sb) + jax.nn.sigmoid(g_ml) * (y_ml @ w_branch_ml)
    return (merged @ w_out).astype(xn.dtype)


def peer(xn, w_query, keys1, keys2, expert_u, expert_v):
    B, S, D = xn.shape
    T = B * S
    xt = xn.reshape(T, D)
    q = (xt @ w_query).astype(jnp.float32).reshape(T, PEER_HEADS, 2, PEER_HALF)
    s1 = jnp.einsum('thd,hkd->thk', q[:, :, 0], keys1.astype(jnp.float32))
    s2 = jnp.einsum('thd,hkd->thk', q[:, :, 1], keys2.astype(jnp.float32))
    v1, i1 = lax.top_k(s1, PEER_TOPK)
    v2, i2 = lax.top_k(s2, PEER_TOPK)
    cand = (v1[..., :, None] + v2[..., None, :]).reshape(T, PEER_HEADS, PEER_TOPK * PEER_TOPK)
    top_s, top_c = lax.top_k(cand, PEER_TOPK)
    e_idx = jnp.take_along_axis(i1, top_c // PEER_TOPK, axis=-1) * PEER_N_KEYS + jnp.take_along_axis(i2, top_c % PEER_TOPK, axis=-1)
    gate = jax.nn.softmax(top_s, axis=-1)
    n_sel = PEER_HEADS * PEER_TOPK
    idx = e_idx.reshape(T // PEER_TOKEN_BLOCK, PEER_TOKEN_BLOCK, n_sel)
    gts = gate.reshape(T // PEER_TOKEN_BLOCK, PEER_TOKEN_BLOCK, n_sel)
    xb = xt.reshape(T // PEER_TOKEN_BLOCK, PEER_TOKEN_BLOCK, D)

    def block_fn(args):
        xc, ic, gc = args
        u = expert_u[ic].astype(jnp.float32)
        act = jax.nn.gelu(jnp.einsum('ckd,cd->ck', u, xc.astype(jnp.float32)), approximate=False)
        return jnp.einsum('ck,ckd->cd', gc * act, expert_v[ic].astype(jnp.float32))

    out = lax.map(block_fn, (xb, idx, gts))
    return out.reshape(B, S, D).astype(xn.dtype)


def setup_inputs(seed: int = 0) -> dict:
    key = jax.random.key(seed)
    ks = jax.random.split(key, 18)
    f32 = jnp.float32
    nrm = lambda k, shape: jax.random.normal(k, shape, f32)
    return {
        'x': nrm(ks[0], (BATCH, SEQ, D_MODEL)),
        'norm_mix_w': 1.0 + 0.02 * nrm(ks[1], (DEPTH, D_MODEL)),
        'w_in': nrm(ks[2], (DEPTH, D_MODEL, IN_COLS)) * D_MODEL ** -0.5,
        'conv_w': nrm(ks[3], (DEPTH, CONV_WIDTH, 2 * ML_WIDTH)) * CONV_WIDTH ** -0.5,
        'conv_b': 0.02 * nrm(ks[4], (DEPTH, 2 * ML_WIDTH)),
        'b_igate': 0.1 * nrm(ks[5], (DEPTH, ML_HEADS)),
        'b_fgate': 3.0 + 3.0 * jax.random.uniform(ks[6], (DEPTH, ML_HEADS), f32),
        'mlstm_norm_w': 1.0 + 0.02 * nrm(ks[7], (DEPTH, ML_WIDTH)),
        'w_branch_sb': nrm(ks[8], (DEPTH, SB_WIDTH, D_MODEL)) * SB_WIDTH ** -0.5,
        'w_branch_ml': nrm(ks[9], (DEPTH, ML_WIDTH, D_MODEL)) * ML_WIDTH ** -0.5,
        'w_out': nrm(ks[10], (DEPTH, D_MODEL, D_MODEL)) * D_MODEL ** -0.5,
        'norm_ffn_w': 1.0 + 0.02 * nrm(ks[11], (DEPTH, D_MODEL)),
        'peer_w_query': nrm(ks[12], (DEPTH, D_MODEL, PEER_HEADS * PEER_QUERY_DIM)) * D_MODEL ** -0.5,
        'peer_keys1': nrm(ks[13], (DEPTH, PEER_HEADS, PEER_N_KEYS, PEER_HALF)) * PEER_HALF ** -0.5,
        'peer_keys2': nrm(ks[14], (DEPTH, PEER_HEADS, PEER_N_KEYS, PEER_HALF)) * PEER_HALF ** -0.5,
        'peer_u': nrm(ks[15], (DEPTH, PEER_N_EXPERTS, D_MODEL)) * D_MODEL ** -0.5,
        'peer_v': 0.5 * nrm(ks[16], (DEPTH, PEER_N_EXPERTS, D_MODEL)),
        'norm_final_w': 1.0 + 0.02 * nrm(ks[17], (D_MODEL,)),
    }


def reference(x, norm_mix_w, w_in, conv_w, conv_b, b_igate, b_fgate, mlstm_norm_w, w_branch_sb, w_branch_ml, w_out, norm_ffn_w, peer_w_query, peer_keys1, peer_keys2, peer_u, peer_v, norm_final_w):
    h = x
    for l in range(DEPTH):
        xn = rmsnorm(h, norm_mix_w[l])
        h = h + hybrid_mixer(xn, w_in[l], conv_w[l], conv_b[l], b_igate[l], b_fgate[l], mlstm_norm_w[l], w_branch_sb[l], w_branch_ml[l], w_out[l]).astype(h.dtype)
        hn = rmsnorm(h, norm_ffn_w[l])
        h = h + peer(hn, peer_w_query[l], peer_keys1[l], peer_keys2[l], peer_u[l], peer_v[l]).astype(h.dtype)
    return rmsnorm(h, norm_final_w)
```

```python
import functools

import jax
import jax.numpy as jnp
from jax import lax
from jax.experimental import pallas as pl
from jax.experimental.pallas import tpu as pltpu

F32 = jnp.float32
BF16 = jnp.bfloat16
I32 = jnp.int32

D_MODEL = 1024
SB_HEADS = 8
SB_HEAD_DIM = 64
SB_WIDTH = SB_HEADS * SB_HEAD_DIM
ML_HEADS = 4
ML_HEAD_DIM = 128
ML_WIDTH = ML_HEADS * ML_HEAD_DIM
CONV_WIDTH = 4
PEER_HEADS = 8
PEER_N_KEYS = 128
PEER_N_EXPERTS = PEER_N_KEYS * PEER_N_KEYS
PEER_HALF = 128
PEER_TOPK = 16
N_SEL = PEER_HEADS * PEER_TOPK
RMS_EPS = 1e-6

LANES = 128
SUBLANES = 8
VMEM_LIMIT = 48 * 1024 * 1024

TM_PROJ = 256
SBA_TQ = 256
SBA_TK = 128
ML_CHUNK = 128
TT_TOPK = 256
TB_PEER = 64
ROWS_PER_EXPERT = 4
TAB_PAD = 4
TAB_ROWS = PEER_N_EXPERTS * ROWS_PER_EXPERT + 2 * TAB_PAD
HI_MASK = -65536


def _params():
    return pltpu.CompilerParams(dimension_semantics=None, vmem_limit_bytes=VMEM_LIMIT)


def _split3(a):
    hi = a.astype(BF16)
    r = a - hi.astype(F32)
    mid = r.astype(BF16)
    lo = (r - mid.astype(F32)).astype(BF16)
    return hi, mid, lo


def _dot_exact_rhs(a_bf16, b_f32):
    out = None
    for piece in _split3(b_f32):
        d = jnp.dot(a_bf16, piece, preferred_element_type=F32)
        out = d if out is None else out + d
    return out


def _dot_exact_lhs(a_f32, b_bf16):
    out = None
    for piece in _split3(a_f32):
        d = jnp.dot(piece, b_bf16, preferred_element_type=F32)
        out = d if out is None else out + d
    return out


def _inproj_kernel(x_ref, nw_ref, wsb, wqk, wv, wo, wg, wif, osb, oqk, ov, oo, og, oif):
    x = x_ref[...]
    ms = jnp.mean(x * x, axis=-1, keepdims=True)
    xb = (x * lax.rsqrt(ms + RMS_EPS) * nw_ref[...]).astype(BF16)
    osb[...] = jnp.dot(xb, wsb[...], preferred_element_type=F32).astype(BF16)
    oqk[...] = jnp.dot(xb, wqk[...], preferred_element_type=F32)
    ov[...] = jnp.dot(xb, wv[...], preferred_element_type=F32).astype(BF16)
    oo[...] = jnp.dot(xb, wo[...], preferred_element_type=F32)
    og[...] = jnp.dot(xb, wg[...], preferred_element_type=F32)
    oif[...] = jnp.dot(xb, wif[...], preferred_element_type=F32)


def _in_proj(xt, nw, wsb, wqk, wv, wo, wg, wif):
    t = xt.shape[0]
    tm = min(TM_PROJ, t)
    full = lambda a: pl.BlockSpec(a.shape, lambda i: (0, 0))
    tok = lambda n: pl.BlockSpec((tm, n), lambda i: (i, 0))
    ws = (wsb, wqk, wv, wo, wg, wif)
    outs = [(wsb.shape[1], BF16), (wqk.shape[1], F32), (wv.shape[1], BF16),
            (wo.shape[1], F32), (wg.shape[1], F32), (wif.shape[1], F32)]
    return pl.pallas_call(
        _inproj_kernel,
        grid=(t // tm,),
        in_specs=[tok(D_MODEL), full(nw)] + [full(w) for w in ws],
        out_specs=[tok(n) for n, _ in outs],
        out_shape=[jax.ShapeDtypeStruct((t, n), dt) for n, dt in outs],
        compiler_params=_params(),
        name="in_proj",
    )(xt, nw, *ws)


def _sba_block(qm, k_ref, v_ref, uu, carry_ref, acc_ref, kb, row0, masked):
    c0 = pl.multiple_of(kb * SBA_TK, SBA_TK)
    kblk = k_ref[pl.ds(c0, SBA_TK), :]
    vblk = v_ref[pl.ds(c0, SBA_TK), :]
    z = lax.dot_general(qm, kblk, (((1,), (1,)), ((), ())), preferred_element_type=F32)
    sp = jnp.maximum(z, 0.0) + jnp.log(1.0 + jnp.exp(-jnp.abs(z)))
    if masked:
        rows = row0 + lax.broadcasted_iota(I32, (SBA_TQ, SBA_TK), 0)
        cols = c0 + lax.broadcasted_iota(I32, (SBA_TQ, SBA_TK), 1)
        causal = cols < rows
        sp = jnp.where(causal, sp, 0.0)
    hi = sp.astype(BF16)
    lo = (sp - hi.astype(F32)).astype(BF16)
    su = jnp.dot(jnp.concatenate([hi, lo], axis=1), uu, preferred_element_type=F32)
    carry = carry_ref[...]
    w = jnp.exp(z - sp - su[:, :SBA_TK] - carry)
    if masked:
        w = jnp.where(causal, w, 0.0)
    acc_ref[...] += jnp.dot(w.astype(BF16), vblk, preferred_element_type=F32)
    carry_ref[...] = carry + su[:, SBA_TK:]


def _sba_kernel(q_ref, k_ref, v_ref, y_ref, carry_ref, acc_ref, *, seq):
    r = lax.broadcasted_iota(I32, (2 * SBA_TK, 2 * SBA_TK), 0) % SBA_TK
    c = lax.broadcasted_iota(I32, (2 * SBA_TK, 2 * SBA_TK), 1)
    uu = jnp.where((c >= SBA_TK) | (r > c), 1.0, 0.0).astype(BF16)
    lane = lax.broadcasted_iota(I32, (SBA_TQ, LANES), 1)
    kper = SBA_TQ // SBA_TK

    def qblock(qi, carry):
        row0 = pl.multiple_of(qi * SBA_TQ, SBA_TQ)
        qf = q_ref[pl.ds(row0, SBA_TQ), :] * jnp.asarray(SB_HEAD_DIM ** -0.5, BF16)
        accs = []
        for hh in range(2):
            head = (lane >= SB_HEAD_DIM) if hh else (lane < SB_HEAD_DIM)
            qm = jnp.where(head, qf, jnp.zeros_like(qf))
            carry_ref[...] = jnp.zeros_like(carry_ref)
            acc_ref[...] = jnp.zeros_like(acc_ref)
            for dd in range(kper):
                kb = qi * kper + (kper - 1 - dd)
                _sba_block(qm, k_ref, v_ref, uu, carry_ref, acc_ref, kb, row0, True)

            def below(kk, c2):
                kb = qi * kper - 1 - kk
                _sba_block(qm, k_ref, v_ref, uu, carry_ref, acc_ref, kb, row0, False)
                return c2

            lax.fori_loop(0, qi * kper, below, 0)
            accs.append(acc_ref[...])
        y = jnp.where(lane < SB_HEAD_DIM, accs[0], accs[1])
        y_ref[pl.ds(row0, SBA_TQ), :] = y.astype(BF16)
        return carry

    lax.fori_loop(0, seq // SBA_TQ, qblock, 0)


def _sba(sb_qkv, batch, seq):
    npair = SB_WIDTH // LANES
    blk = lambda off: pl.BlockSpec((None, seq, LANES), lambda b, p: (b, 0, off + p))
    return pl.pallas_call(
        functools.partial(_sba_kernel, seq=seq),
        grid=(batch, npair),
        in_specs=[blk(0), blk(npair), blk(2 * npair)],
        out_specs=pl.BlockSpec((None, seq, LANES), lambda b, p: (b, 0, p)),
        out_shape=jax.ShapeDtypeStruct((batch, seq, SB_WIDTH), BF16),
        scratch_shapes=[pltpu.VMEM((SBA_TQ, SBA_TK), F32), pltpu.VMEM((SBA_TQ, LANES), F32)],
        compiler_params=_params(),
        name="sba",
    )(sb_qkv, sb_qkv, sb_qkv)


def _conv_silu(x, w_ref, b_ref, row):
    y = x * w_ref[CONV_WIDTH - 1:CONV_WIDTH, :] + b_ref[...]
    for k in range(1, CONV_WIDTH):
        shifted = jnp.where(row >= k, pltpu.roll(x, k, axis=0), 0.0)
        y = y + shifted * w_ref[CONV_WIDTH - 1 - k:CONV_WIDTH - k, :]
    return y * jax.nn.sigmoid(y)


def _log_sigmoid(x):
    return -(jnp.maximum(-x, 0.0) + jnp.log1p(jnp.exp(-jnp.abs(x))))


def _mlstm_kernel(q_ref, k_ref, v_ref, o_ref, if_ref, cwq, cwk, cbq, cbk, ifb_ref, nw_ref, y_ref,
                  qs_ref, ks_ref, ic_ref, lf_ref, h_ref, c_ref, n_ref, m_ref, *, seq):
    hd = pl.program_id(1)
    L = ML_CHUNK
    row = lax.broadcasted_iota(I32, (seq, LANES), 0)
    qs_ref[...] = _conv_silu(q_ref[...], cwq, cbq, row).astype(BF16)
    ks_ref[...] = (_conv_silu(k_ref[...], cwk, cbk, row) * (ML_HEAD_DIM ** -0.5)).astype(BF16)

    gates = if_ref[...] + ifb_ref[...]
    src = lax.broadcasted_iota(I32, (LANES, LANES), 0)
    sel_i = jnp.where(src == hd, 1.0, 0.0).astype(BF16)
    sel_f = jnp.where(src == hd + ML_HEADS, 1.0, 0.0).astype(BF16)
    ic_ref[...] = _dot_exact_lhs(gates, sel_i)
    lf_ref[...] = _log_sigmoid(_dot_exact_lhs(gates, sel_f))

    rr = lax.broadcasted_iota(I32, (L, L), 0)
    cc = lax.broadcasted_iota(I32, (L, L), 1)
    tril = cc <= rr
    tril_b = jnp.where(tril, 1.0, 0.0).astype(BF16)
    eye = jnp.where(cc == rr, 1.0, 0.0)
    ones_b = jnp.ones((L, L), BF16)

    c_ref[...] = jnp.zeros_like(c_ref)
    n_ref[...] = jnp.zeros_like(n_ref)
    m_ref[...] = jnp.zeros_like(m_ref)

    def chunk(ci, carry):
        r0 = pl.multiple_of(ci * L, L)
        qc = qs_ref[pl.ds(r0, L), :]
        kc = ks_ref[pl.ds(r0, L), :]
        vc = v_ref[pl.ds(r0, L), :]
        lf = lf_ref[pl.ds(r0, L), :]
        ic = ic_ref[pl.ds(r0, L), :]
        m = m_ref[...]
        n = n_ref[...]
        cmat = c_ref[...]
        b = _dot_exact_rhs(tril_b, lf)
        drow = _dot_exact_rhs(ones_b, eye * (ic - b))
        dmat = jnp.where(tril, b + drow, -jnp.inf)
        inter = b + m
        m_t = jnp.maximum(inter, jnp.max(dmat, axis=-1, keepdims=True))
        w_intra = jnp.exp(dmat - m_t)
        w_inter = jnp.exp(inter - m_t)
        s = lax.dot_general(qc, kc, (((1,), (1,)), ((), ())), preferred_element_type=F32) * w_intra
        num = jnp.dot(s.astype(BF16), vc, preferred_element_type=F32)
        num = num + w_inter * jnp.dot(qc, cmat.astype(BF16), preferred_element_type=F32)
        qn = jnp.sum(qc.astype(F32) * n, axis=-1, keepdims=True)
        den = jnp.sum(s, axis=-1, keepdims=True) + w_inter * qn
        h_ref[pl.ds(r0, L), :] = num / jnp.maximum(jnp.abs(den), jnp.exp(-m_t))
        b_last = b[L - 1:L, :]
        g = b_last - b + ic
        m_new = jnp.maximum(b_last + m, jnp.max(g, axis=0, keepdims=True))
        decay = jnp.exp(b_last + m - m_new)
        kw = kc.astype(F32) * jnp.exp(g - m_new)
        c_ref[...] = decay * cmat + lax.dot_general(
            kw.astype(BF16), vc, (((0,), (0,)), ((), ())), preferred_element_type=F32)
        n_ref[...] = decay * n + jnp.sum(kw, axis=0, keepdims=True)
        m_ref[...] = m_new
        return carry

    lax.fori_loop(0, seq // L, chunk, 0)

    h = h_ref[...]
    hn = h * lax.rsqrt(jnp.mean(h * h, axis=-1, keepdims=True) + RMS_EPS) * nw_ref[...]
    y_ref[...] = (jax.nn.sigmoid(o_ref[...]) * hn).astype(BF16)


def _mlstm(ml_qk, ml_v, ml_o, ml_if, conv_w, conv_b, ifb, nw, batch, seq):
    nh = ML_HEADS
    col = lambda off: pl.BlockSpec((None, seq, LANES), lambda b, h: (b, 0, off + h))
    wcol = lambda rows, off: pl.BlockSpec((rows, LANES), lambda b, h: (0, off + h))
    return pl.pallas_call(
        functools.partial(_mlstm_kernel, seq=seq),
        grid=(batch, nh),
        in_specs=[col(0), col(nh), col(0), col(0),
                  pl.BlockSpec((None, seq, LANES), lambda b, h: (b, 0, 0)),
                  wcol(CONV_WIDTH, 0), wcol(CONV_WIDTH, nh), wcol(1, 0), wcol(1, nh),
                  pl.BlockSpec((1, LANES), lambda b, h: (0, 0)), wcol(1, 0)],
        out_specs=col(0),
        out_shape=jax.ShapeDtypeStruct((batch, seq, ML_WIDTH), BF16),
        scratch_shapes=[pltpu.VMEM((seq, LANES), BF16), pltpu.VMEM((seq, LANES), BF16),
                        pltpu.VMEM((seq, LANES), F32), pltpu.VMEM((seq, LANES), F32),
                        pltpu.VMEM((seq, LANES), F32), pltpu.VMEM((LANES, LANES), F32),
                        pltpu.VMEM((1, LANES), F32), pltpu.VMEM((1, LANES), F32)],
        compiler_params=_params(),
        name="mlstm",
    )(ml_qk, ml_qk, ml_v, ml_o, ml_if, conv_w, conv_w, conv_b, conv_b, ifb, nw)


def _merge_kernel(ysb, yml, g_ref, x_ref, wsb, wml, wout, nw_ref, wq, h_ref, hn_ref, q_ref):
    a = jnp.dot(ysb[...], wsb[...], preferred_element_type=F32)
    b = jnp.dot(yml[...], wml[...], preferred_element_type=F32)
    g = g_ref[...]
    merged = jax.nn.sigmoid(g[:, :D_MODEL]) * a + jax.nn.sigmoid(g[:, D_MODEL:]) * b
    h = x_ref[...] + jnp.dot(merged.astype(BF16), wout[...], preferred_element_type=F32)
    h_ref[...] = h
    hn = h * lax.rsqrt(jnp.mean(h * h, axis=-1, keepdims=True) + RMS_EPS) * nw_ref[...]
    hn_ref[...] = hn
    q_ref[...] = jnp.dot(hn.astype(BF16), wq[...], preferred_element_type=F32).astype(BF16)


def _merge(ysb, yml, gates, xt, wsb, wml, wout, nw, wq):
    t = xt.shape[0]
    tm = min(TM_PROJ, t)
    full = lambda a: pl.BlockSpec(a.shape, lambda i: (0, 0))
    tok = lambda n: pl.BlockSpec((tm, n), lambda i: (i, 0))
    nq = wq.shape[1]
    return pl.pallas_call(
        _merge_kernel,
        grid=(t // tm,),
        in_specs=[tok(SB_WIDTH), tok(ML_WIDTH), tok(2 * D_MODEL), tok(D_MODEL),
                  full(wsb), full(wml), full(wout), full(nw), full(wq)],
        out_specs=[tok(D_MODEL), tok(D_MODEL), tok(nq)],
        out_shape=[jax.ShapeDtypeStruct((t, D_MODEL), F32), jax.ShapeDtypeStruct((t, D_MODEL), F32),
                   jax.ShapeDtypeStruct((t, nq), BF16)],
        compiler_params=_params(),
        name="merge",
    )(ysb, yml, gates, xt, wsb, wml, wout, nw, wq)


_BIG = 3.0e38


def _extract_top(vals, payload, val_ref, pay_ref):
    for it in range(PEER_TOPK):
        m = jnp.max(vals, axis=0, keepdims=True)
        p = jnp.min(jnp.where(vals == m, payload, _BIG), axis=0, keepdims=True)
        val_ref[it:it + 1, :] = m
        pay_ref[it:it + 1, :] = p
        vals = jnp.where(payload == p, -jnp.inf, vals)


def _topk_kernel(q_ref, k1_ref, k2_ref, off_ref, gate_ref, v1, i1, v2, i2, tv, te, gall, eall, *, tt):
    dn = (((1,), (1,)), ((), ()))
    key_iota = lax.broadcasted_iota(I32, (PEER_N_KEYS, tt), 0).astype(F32)
    a8 = lax.broadcasted_iota(I32, (SUBLANES, tt), 0)
    for h in range(PEER_HEADS):
        q1 = q_ref[:, 2 * h * PEER_HALF:(2 * h + 1) * PEER_HALF]
        q2 = q_ref[:, (2 * h + 1) * PEER_HALF:(2 * h + 2) * PEER_HALF]
        s1 = lax.dot_general(k1_ref[h], q1, dn, preferred_element_type=F32)
        s2 = lax.dot_general(k2_ref[h], q2, dn, preferred_element_type=F32)
        _extract_top(s1, key_iota, v1, i1)
        _extract_top(s2, key_iota, v2, i2)
        va, vb = v1[...], v2[...]
        ea, eb = i1[...] * float(PEER_N_KEYS), i2[...]
        lo8 = slice(0, SUBLANES)
        hi8 = slice(SUBLANES, 2 * SUBLANES)
        cv = [va[0:1] + vb, va[1:2] + vb[lo8], va[hi8] + vb[0:1]]
        ce = [ea[0:1] + eb, ea[1:2] + eb[lo8], ea[hi8] + eb[0:1]]
        for bcol, amax in ((0, 7), (1, 7), (2, 4), (3, 3), (4, 2)):
            ok = (a8 >= 2) & (a8 <= amax)
            cv.append(jnp.where(ok, va[lo8] + vb[bcol:bcol + 1], -jnp.inf))
            ce.append(ea[lo8] + eb[bcol:bcol + 1])
        _extract_top(jnp.concatenate(cv, axis=0), jnp.concatenate(ce, axis=0), tv, te)
        sc = tv[...]
        ex = jnp.exp(sc - jnp.max(sc, axis=0, keepdims=True))
        gall[h * PEER_TOPK:(h + 1) * PEER_TOPK, :] = ex / jnp.sum(ex, axis=0, keepdims=True)
        eall[h * PEER_TOPK:(h + 1) * PEER_TOPK, :] = te[...]
    gate_ref[...] = jnp.transpose(gall[...])
    off_ref[...] = (jnp.transpose(eall[...]) * float(ROWS_PER_EXPERT) + float(TAB_PAD)).astype(I32)


def _topk(q, k1, k2):
    t = q.shape[0]
    tt = min(TT_TOPK, t)
    kspec = pl.BlockSpec(k1.shape, lambda i: (0, 0, 0))
    small = lambda: pltpu.VMEM((PEER_TOPK, tt), F32)
    return pl.pallas_call(
        functools.partial(_topk_kernel, tt=tt),
        grid=(t // tt,),
        in_specs=[pl.BlockSpec((tt, q.shape[1]), lambda i: (i, 0)), kspec, kspec],
        out_specs=[pl.BlockSpec((tt, N_SEL), lambda i: (i, 0)), pl.BlockSpec((tt, N_SEL), lambda i: (i, 0))],
        out_shape=[jax.ShapeDtypeStruct((t, N_SEL), I32), jax.ShapeDtypeStruct((t, N_SEL), F32)],
        scratch_shapes=[small(), small(), small(), small(), small(), small(),
                        pltpu.VMEM((N_SEL, tt), F32), pltpu.VMEM((N_SEL, tt), F32)],
        compiler_params=_params(),
        name="topk",
    )(q, k1, k2)


def _load_table(tab_hbm, tab_vmem, sem):
    @pl.when(pl.program_id(0) == 0)
    def _():
        cp = pltpu.make_async_copy(tab_hbm, tab_vmem, sem)
        cp.start()
        cp.wait()


def _gather_pair(tab_vmem, oa, ob, low4):
    a = tab_vmem[pl.ds(oa, SUBLANES), :]
    b = tab_vmem[pl.ds(ob - ROWS_PER_EXPERT, SUBLANES), :]
    return jnp.where(low4, a, b)


def _unpack(bits):
    lo = pltpu.bitcast(jnp.left_shift(bits, 16), F32)
    hi = pltpu.bitcast(bits & HI_MASK, F32)
    return lo, hi


def _merge_halves(a, b, m, k):
    r = pltpu.roll(b, k, axis=0)
    c = jnp.where(m, a, r)
    w = jnp.where(m, r, a)
    return c + pltpu.roll(w, SUBLANES - k, axis=0)


def _peer_u_kernel(off_ref, x_ref, gate_ref, tab_hbm, w_ref, tab_vmem, sem, *, tb):
    _load_table(tab_hbm, tab_vmem, sem)
    sub = lax.broadcasted_iota(I32, (SUBLANES, LANES), 0)
    low4 = sub < 4
    m2 = (sub & 2) == 0
    m1 = (sub & 1) == 0
    dn = (((1,), (1,)), ((), ()))

    def group(g, carry):
        acc = jnp.zeros((SUBLANES, N_SEL), F32)
        for tt in range(SUBLANES):
            t = g * SUBLANES + tt
            r16 = pl.multiple_of(t * 16, 16)
            xlo = x_ref[pl.ds(r16, SUBLANES), :]
            xhi = x_ref[pl.ds(r16 + SUBLANES, SUBLANES), :]
            qs = []
            for jg in range(N_SEL // 8):
                ss = []
                for jj in range(4):
                    ja = jg * 8 + (0, 2, 1, 3)[jj]
                    lo, hi = _unpack(_gather_pair(tab_vmem, off_ref[t, ja], off_ref[t, ja + 4], low4))
                    ss.append(lo * xlo + hi * xhi)
                u01 = _merge_halves(ss[0], ss[1], m2, 2)
                u23 = _merge_halves(ss[2], ss[3], m2, 2)
                qs.append(_merge_halves(u01, u23, m1, 1))
            qm = jnp.concatenate(qs, axis=0)
            hi = qm.astype(BF16)
            lo = (qm - hi.astype(F32)).astype(BF16)
            pick = jnp.where(sub == tt, 1.0, 0.0).astype(BF16)
            acc = acc + lax.dot_general(pick, hi, dn, preferred_element_type=F32)
            acc = acc + lax.dot_general(pick, lo, dn, preferred_element_type=F32)
        r0 = pl.multiple_of(g * SUBLANES, SUBLANES)
        gelu = 0.5 * acc * (1.0 + lax.erf(acc * (2.0 ** -0.5)))
        w_ref[pl.ds(r0, SUBLANES), :] = gate_ref[pl.ds(r0, SUBLANES), :] * gelu
        return carry

    lax.fori_loop(0, tb // SUBLANES, group, 0)


def _peer_u(off, x16, gate, tab):
    t = off.shape[0]
    tb = min(TB_PEER, t)
    return pl.pallas_call(
        functools.partial(_peer_u_kernel, tb=tb),
        grid=(t // tb,),
        in_specs=[pl.BlockSpec((tb, N_SEL), lambda i: (i, 0), memory_space=pltpu.SMEM),
                  pl.BlockSpec((tb * 16, LANES), lambda i: (i, 0)),
                  pl.BlockSpec((tb, N_SEL), lambda i: (i, 0)),
                  pl.BlockSpec(memory_space=pl.ANY)],
        out_specs=pl.BlockSpec((tb, N_SEL), lambda i: (i, 0)),
        out_shape=jax.ShapeDtypeStruct((t, N_SEL), F32),
        scratch_shapes=[pltpu.VMEM((TAB_ROWS, LANES), I32), pltpu.SemaphoreType.DMA(())],
        compiler_params=_params(),
        name="peer_u",
    )(off, x16, gate, tab)


def _peer_v_kernel(off_ref, w_ref, h_ref, nw_ref, tab_hbm, out_ref, tab_vmem, sem, *, tb):
    _load_table(tab_hbm, tab_vmem, sem)
    sub = lax.broadcasted_iota(I32, (SUBLANES, LANES), 0)
    low4 = sub < 4
    nacc = 4

    def token(t, carry):
        acc_lo = [jnp.zeros((SUBLANES, LANES), F32) for _ in range(nacc)]
        acc_hi = [jnp.zeros((SUBLANES, LANES), F32) for _ in range(nacc)]
        for p in range(N_SEL // 2):
            ja, jb = 2 * p, 2 * p + 1
            lo, hi = _unpack(_gather_pair(tab_vmem, off_ref[t, ja], off_ref[t, jb], low4))
            w2 = jnp.where(low4, w_ref[t, ja], w_ref[t, jb])
            acc_lo[p % nacc] = acc_lo[p % nacc] + lo * w2
            acc_hi[p % nacc] = acc_hi[p % nacc] + hi * w2
        s_lo = (acc_lo[0] + acc_lo[1]) + (acc_lo[2] + acc_lo[3])
        s_hi = (acc_hi[0] + acc_hi[1]) + (acc_hi[2] + acc_hi[3])
        s_lo = s_lo + pltpu.roll(s_lo, 4, axis=0)
        s_hi = s_hi + pltpu.roll(s_hi, 4, axis=0)
        r8 = pl.multiple_of(t * SUBLANES, SUBLANES)
        y = h_ref[pl.ds(r8, SUBLANES), :] + jnp.where(low4, s_lo, s_hi)
        ss = jnp.sum(jnp.sum(y * y, axis=1, keepdims=True), axis=0, keepdims=True)
        out_ref[pl.ds(r8, SUBLANES), :] = y * lax.rsqrt(ss * (1.0 / D_MODEL) + RMS_EPS) * nw_ref[...]
        return carry

    lax.fori_loop(0, tb, token, 0)


def _peer_v(off, w, h8, nw8, tab):
    t = off.shape[0]
    tb = min(TB_PEER, t)
    smem = lambda: pl.BlockSpec((tb, N_SEL), lambda i: (i, 0), memory_space=pltpu.SMEM)
    return pl.pallas_call(
        functools.partial(_peer_v_kernel, tb=tb),
        grid=(t // tb,),
        in_specs=[smem(), smem(),
                  pl.BlockSpec((tb * SUBLANES, LANES), lambda i: (i, 0)),
                  pl.BlockSpec((SUBLANES, LANES), lambda i: (0, 0)),
                  pl.BlockSpec(memory_space=pl.ANY)],
        out_specs=pl.BlockSpec((tb * SUBLANES, LANES), lambda i: (i, 0)),
        out_shape=jax.ShapeDtypeStruct((t * SUBLANES, LANES), F32),
        scratch_shapes=[pltpu.VMEM((TAB_ROWS, LANES), I32), pltpu.SemaphoreType.DMA(())],
        compiler_params=_params(),
        name="peer_v",
    )(off, w, h8, nw8, tab)


def _pack_table(w):
    bits = lax.bitcast_convert_type(w.astype(BF16), jnp.uint16).astype(jnp.uint32)
    half = D_MODEL // 2
    packed = lax.bitcast_convert_type(bits[:, :half] | (bits[:, half:] << 16), I32)
    rows = packed.reshape(PEER_N_EXPERTS * ROWS_PER_EXPERT, LANES)
    return jnp.pad(rows, ((TAB_PAD, TAB_PAD), (0, 0)))


def kernel(x, norm_mix_w, w_in, conv_w, conv_b, b_igate, b_fgate, mlstm_norm_w, w_branch_sb, w_branch_ml, w_out, norm_ffn_w, peer_w_query, peer_keys1, peer_keys2, peer_u, peer_v, norm_final_w):
    batch, seq, d = x.shape
    t = batch * seq
    assert d == D_MODEL and w_in.shape[0] == 1
    assert seq % SBA_TQ == 0 and seq % ML_CHUNK == 0 and t % TM_PROJ == 0 and t % TT_TOPK == 0
    xt = x.reshape(t, d)

    o_sb = 3 * SB_WIDTH
    o_qk = o_sb + 2 * ML_WIDTH
    o_v = o_qk + ML_WIDTH
    o_o = o_v + ML_WIDTH
    o_if = o_o + 2 * ML_HEADS
    w = w_in[0]
    wb = lambda a: a.astype(BF16)
    w_if = jnp.pad(w[:, o_o:o_if], ((0, 0), (0, LANES - 2 * ML_HEADS)))
    sb_qkv, ml_qk, ml_v, ml_o, gates, ml_if = _in_proj(
        xt, norm_mix_w[0].reshape(1, d), wb(w[:, :o_sb]), wb(w[:, o_sb:o_qk]), wb(w[:, o_qk:o_v]),
        wb(w[:, o_v:o_o]), wb(w[:, o_if:]), wb(w_if))

    y_sb = _sba(sb_qkv.reshape(batch, seq, 3 * SB_WIDTH), batch, seq)

    ifb = jnp.pad(jnp.concatenate([b_igate[0], b_fgate[0]]), (0, LANES - 2 * ML_HEADS)).reshape(1, LANES)
    y_ml = _mlstm(ml_qk.reshape(batch, seq, 2 * ML_WIDTH), ml_v.reshape(batch, seq, ML_WIDTH),
                  ml_o.reshape(batch, seq, ML_WIDTH), ml_if.reshape(batch, seq, LANES),
                  conv_w[0], conv_b[0].reshape(1, 2 * ML_WIDTH), ifb.astype(F32),
                  mlstm_norm_w[0].reshape(1, ML_WIDTH), batch, seq)

    h, hn, q = _merge(y_sb.reshape(t, SB_WIDTH), y_ml.reshape(t, ML_WIDTH), gates, xt,
                      wb(w_branch_sb[0]), wb(w_branch_ml[0]), wb(w_out[0]),
                      norm_ffn_w[0].reshape(1, d), wb(peer_w_query[0]))

    off, gate = _topk(q, wb(peer_keys1[0]), wb(peer_keys2[0]))

    half = d // 2
    xl = hn[:, :half].reshape(t, ROWS_PER_EXPERT, LANES)
    xh = hn[:, half:].reshape(t, ROWS_PER_EXPERT, LANES)
    x16 = jnp.concatenate([xl, xl, xh, xh], axis=1).reshape(t * 16, LANES)
    wsel = _peer_u(off, x16, gate, _pack_table(peer_u[0]))

    nw8 = norm_final_w.reshape(SUBLANES, LANES)
    out = _peer_v(off, wsel, h.reshape(t * SUBLANES, LANES), nw8, _pack_table(peer_v[0]))
    return out.reshape(batch, seq, d)
```

```python
import functools

import jax
import jax.numpy as jnp
from jax import lax
from jax.experimental import pallas as pl
from jax.experimental.pallas import tpu as pltpu

F32 = jnp.float32
BF16 = jnp.bfloat16
I32 = jnp.int32

D_MODEL = 1024
SB_HEADS = 8
SB_HEAD_DIM = 64
SB_WIDTH = SB_HEADS * SB_HEAD_DIM
ML_HEADS = 4
ML_HEAD_DIM = 128
ML_WIDTH = ML_HEADS * ML_HEAD_DIM
CONV_WIDTH = 4
PEER_HEADS = 8
PEER_N_KEYS = 128
PEER_N_EXPERTS = PEER_N_KEYS * PEER_N_KEYS
PEER_HALF = 128
PEER_TOPK = 16
N_SEL = PEER_HEADS * PEER_TOPK
RMS_EPS = 1e-6

LANES = 128
SUBLANES = 8
VMEM_LIMIT = 48 * 1024 * 1024

TM_PROJ = 256
SBA_TQ = 256
SBA_TK = 128
SBA_TK2 = 2 * SBA_TK
ML_CHUNK = 128
TT_TOPK = 256
PEER_HALF_TOK = 128
TB_PEER = 2 * PEER_HALF_TOK
PEER_GROUP = 16
EXPERT_ROWS = SUBLANES
WORD_ROWS = EXPERT_ROWS // 2
GATHER_ROWS = N_SEL * EXPERT_ROWS
TAB_ROWS = PEER_N_EXPERTS * WORD_ROWS


def _params():
    return pltpu.CompilerParams(dimension_semantics=None, vmem_limit_bytes=VMEM_LIMIT)


def _split3(a):
    hi = a.astype(BF16)
    r = a - hi.astype(F32)
    mid = r.astype(BF16)
    lo = (r - mid.astype(F32)).astype(BF16)
    return hi, mid, lo


def _split2(a):
    hi = a.astype(BF16)
    return hi, (a - hi.astype(F32)).astype(BF16)


def _dot_exact_rhs(a_bf16, b_f32):
    out = None
    for piece in _split3(b_f32):
        d = jnp.dot(a_bf16, piece, preferred_element_type=F32)
        out = d if out is None else out + d
    return out


def _dot_exact_lhs(a_f32, b_bf16):
    out = None
    for piece in _split3(a_f32):
        d = jnp.dot(piece, b_bf16, preferred_element_type=F32)
        out = d if out is None else out + d
    return out


def _inproj_kernel(x_ref, nw_ref, wsb, wqk, wv, wo, wg, wif, osb, oqk, ov, oo, og, oif):
    x = x_ref[...]
    ms = jnp.mean(x * x, axis=-1, keepdims=True)
    xb = (x * lax.rsqrt(ms + RMS_EPS) * nw_ref[...]).astype(BF16)
    osb[...] = jnp.dot(xb, wsb[...], preferred_element_type=F32).astype(BF16)
    oqk[...] = jnp.dot(xb, wqk[...], preferred_element_type=F32)
    ov[...] = jnp.dot(xb, wv[...], preferred_element_type=F32).astype(BF16)
    oo[...] = jnp.dot(xb, wo[...], preferred_element_type=F32)
    og[...] = jnp.dot(xb, wg[...], preferred_element_type=F32)
    oif[...] = jnp.dot(xb, wif[...], preferred_element_type=F32)


def _in_proj(xt, nw, wsb, wqk, wv, wo, wg, wif):
    t = xt.shape[0]
    tm = min(TM_PROJ, t)
    full = lambda a: pl.BlockSpec(a.shape, lambda i: (0, 0))
    tok = lambda n: pl.BlockSpec((tm, n), lambda i: (i, 0))
    ws = (wsb, wqk, wv, wo, wg, wif)
    outs = [(wsb.shape[1], BF16), (wqk.shape[1], F32), (wv.shape[1], BF16),
            (wo.shape[1], F32), (wg.shape[1], F32), (wif.shape[1], F32)]
    return pl.pallas_call(
        _inproj_kernel,
        grid=(t // tm,),
        in_specs=[tok(D_MODEL), full(nw)] + [full(w) for w in ws],
        out_specs=[tok(n) for n, _ in outs],
        out_shape=[jax.ShapeDtypeStruct((t, n), dt) for n, dt in outs],
        compiler_params=_params(),
        name="in_proj",
    )(xt, nw, *ws)


def _sba_step(qms, k_ref, v_ref, uu, carry_ref, acc_ref, kb2, row0, masked):
    c0 = pl.multiple_of(kb2 * SBA_TK2, SBA_TK2)
    kblk = k_ref[pl.ds(c0, SBA_TK2), :]
    vblk = v_ref[pl.ds(c0, SBA_TK2), :]
    if masked:
        rows = row0 + lax.broadcasted_iota(I32, (SBA_TQ, SBA_TK2), 0)
        cols = c0 + lax.broadcasted_iota(I32, (SBA_TQ, SBA_TK2), 1)
        causal = cols < rows
    for hh in range(2):
        z = lax.dot_general(qms[hh], kblk, (((1,), (1,)), ((), ())), preferred_element_type=F32)
        sp = jnp.maximum(z, 0.0) + jnp.log(1.0 + jnp.exp(-jnp.abs(z)))
        if masked:
            sp = jnp.where(causal, sp, 0.0)
        hi, lo = _split2(sp)
        su_lo = jnp.dot(jnp.concatenate([hi[:, :SBA_TK], lo[:, :SBA_TK]], axis=1), uu,
                        preferred_element_type=F32)
        su_hi = jnp.dot(jnp.concatenate([hi[:, SBA_TK:], lo[:, SBA_TK:]], axis=1), uu,
                        preferred_element_type=F32)
        carry = carry_ref[hh]
        later = jnp.concatenate([su_lo[:, :SBA_TK] + su_hi[:, SBA_TK:], su_hi[:, :SBA_TK]], axis=1)
        w = jnp.exp(z - sp - later - jnp.concatenate([carry, carry], axis=1))
        if masked:
            w = jnp.where(causal, w, 0.0)
        acc_ref[hh] += jnp.dot(w.astype(BF16), vblk, preferred_element_type=F32)
        carry_ref[hh] = carry + su_lo[:, SBA_TK:] + su_hi[:, SBA_TK:]


def _sba_kernel(q_ref, k_ref, v_ref, y_ref, carry_ref, acc_ref, *, seq):
    r = lax.broadcasted_iota(I32, (SBA_TK2, SBA_TK2), 0) % SBA_TK
    c = lax.broadcasted_iota(I32, (SBA_TK2, SBA_TK2), 1)
    uu = jnp.where((c >= SBA_TK) | (r > c), 1.0, 0.0).astype(BF16)
    lane = lax.broadcasted_iota(I32, (SBA_TQ, LANES), 1)
    assert SBA_TQ == SBA_TK2

    def qblock(qi, carry):
        row0 = pl.multiple_of(qi * SBA_TQ, SBA_TQ)
        qf = q_ref[pl.ds(row0, SBA_TQ), :] * jnp.asarray(SB_HEAD_DIM ** -0.5, BF16)
        zero = jnp.zeros_like(qf)
        qms = [jnp.where(lane < SB_HEAD_DIM, qf, zero), jnp.where(lane >= SB_HEAD_DIM, qf, zero)]
        carry_ref[...] = jnp.zeros_like(carry_ref)
        acc_ref[...] = jnp.zeros_like(acc_ref)
        _sba_step(qms, k_ref, v_ref, uu, carry_ref, acc_ref, qi, row0, True)

        def below(kk, c2):
            for d in range(2):
                _sba_step(qms, k_ref, v_ref, uu, carry_ref, acc_ref, qi - 1 - 2 * kk - d, row0, False)
            return c2

        lax.fori_loop(0, qi // 2, below, 0)

        @pl.when(qi % 2 == 1)
        def _():
            _sba_step(qms, k_ref, v_ref, uu, carry_ref, acc_ref, 0, row0, False)

        y = jnp.where(lane < SB_HEAD_DIM, acc_ref[0], acc_ref[1])
        y_ref[pl.ds(row0, SBA_TQ), :] = y.astype(BF16)
        return carry

    lax.fori_loop(0, seq // SBA_TQ, qblock, 0)


def _sba(sb_qkv, batch, seq):
    npair = SB_WIDTH // LANES
    blk = lambda off: pl.BlockSpec((None, seq, LANES), lambda b, p: (b, 0, off + p))
    return pl.pallas_call(
        functools.partial(_sba_kernel, seq=seq),
        grid=(batch, npair),
        in_specs=[blk(0), blk(npair), blk(2 * npair)],
        out_specs=pl.BlockSpec((None, seq, LANES), lambda b, p: (b, 0, p)),
        out_shape=jax.ShapeDtypeStruct((batch, seq, SB_WIDTH), BF16),
        scratch_shapes=[pltpu.VMEM((2, SBA_TQ, SBA_TK), F32), pltpu.VMEM((2, SBA_TQ, LANES), F32)],
        compiler_params=_params(),
        name="sba",
    )(sb_qkv, sb_qkv, sb_qkv)


def _conv_silu(x, w_ref, b_ref, row):
    y = x * w_ref[CONV_WIDTH - 1:CONV_WIDTH, :] + b_ref[...]
    for k in range(1, CONV_WIDTH):
        shifted = jnp.where(row >= k, pltpu.roll(x, k, axis=0), 0.0)
        y = y + shifted * w_ref[CONV_WIDTH - 1 - k:CONV_WIDTH - k, :]
    return y * jax.nn.sigmoid(y)


def _log_sigmoid(x):
    return -(jnp.maximum(-x, 0.0) + jnp.log1p(jnp.exp(-jnp.abs(x))))


def _mlstm_kernel(q_ref, k_ref, v_ref, o_ref, if_ref, cwq, cwk, cbq, cbk, ifb_ref, nw_ref, y_ref,
                  qs_ref, ks_ref, ic_ref, lf_ref, h_ref, c_ref, n_ref, m_ref, *, seq):
    hd = pl.program_id(1)
    L = ML_CHUNK
    row = lax.broadcasted_iota(I32, (seq, LANES), 0)
    qs_ref[...] = _conv_silu(q_ref[...], cwq, cbq, row).astype(BF16)
    ks_ref[...] = (_conv_silu(k_ref[...], cwk, cbk, row) * (ML_HEAD_DIM ** -0.5)).astype(BF16)

    gates = if_ref[...] + ifb_ref[...]
    src = lax.broadcasted_iota(I32, (LANES, LANES), 0)
    sel_i = jnp.where(src == hd, 1.0, 0.0).astype(BF16)
    sel_f = jnp.where(src == hd + ML_HEADS, 1.0, 0.0).astype(BF16)
    ic_ref[...] = _dot_exact_lhs(gates, sel_i)
    lf_ref[...] = _log_sigmoid(_dot_exact_lhs(gates, sel_f))

    rr = lax.broadcasted_iota(I32, (L, L), 0)
    cc = lax.broadcasted_iota(I32, (L, L), 1)
    tril = cc <= rr
    tril_b = jnp.where(tril, 1.0, 0.0).astype(BF16)
    eye = jnp.where(cc == rr, 1.0, 0.0)
    ones_b = jnp.ones((L, L), BF16)

    c_ref[...] = jnp.zeros_like(c_ref)
    n_ref[...] = jnp.zeros_like(n_ref)
    m_ref[...] = jnp.zeros_like(m_ref)

    def chunk(ci, carry):
        r0 = pl.multiple_of(ci * L, L)
        qc = qs_ref[pl.ds(r0, L), :]
        kc = ks_ref[pl.ds(r0, L), :]
        vc = v_ref[pl.ds(r0, L), :]
        lf = lf_ref[pl.ds(r0, L), :]
        ic = ic_ref[pl.ds(r0, L), :]
        m = m_ref[...]
        n = n_ref[...]
        cmat = c_ref[...]
        b = _dot_exact_rhs(tril_b, lf)
        drow = _dot_exact_rhs(ones_b, eye * (ic - b))
        dmat = jnp.where(tril, b + drow, -jnp.inf)
        inter = b + m
        m_t = jnp.maximum(inter, jnp.max(dmat, axis=-1, keepdims=True))
        w_intra = jnp.exp(dmat - m_t)
        w_inter = jnp.exp(inter - m_t)
        s = lax.dot_general(qc, kc, (((1,), (1,)), ((), ())), preferred_element_type=F32) * w_intra
        num = jnp.dot(s.astype(BF16), vc, preferred_element_type=F32)
        num = num + w_inter * jnp.dot(qc, cmat.astype(BF16), preferred_element_type=F32)
        qn = jnp.sum(qc.astype(F32) * n, axis=-1, keepdims=True)
        den = jnp.sum(s, axis=-1, keepdims=True) + w_inter * qn
        h_ref[pl.ds(r0, L), :] = num / jnp.maximum(jnp.abs(den), jnp.exp(-m_t))
        b_last = b[L - 1:L, :]
        g = b_last - b + ic
        m_new = jnp.maximum(b_last + m, jnp.max(g, axis=0, keepdims=True))
        decay = jnp.exp(b_last + m - m_new)
        kw = kc.astype(F32) * jnp.exp(g - m_new)
        c_ref[...] = decay * cmat + lax.dot_general(
            kw.astype(BF16), vc, (((0,), (0,)), ((), ())), preferred_element_type=F32)
        n_ref[...] = decay * n + jnp.sum(kw, axis=0, keepdims=True)
        m_ref[...] = m_new
        return carry

    lax.fori_loop(0, seq // L, chunk, 0)

    h = h_ref[...]
    hn = h * lax.rsqrt(jnp.mean(h * h, axis=-1, keepdims=True) + RMS_EPS) * nw_ref[...]
    y_ref[...] = (jax.nn.sigmoid(o_ref[...]) * hn).astype(BF16)


def _mlstm(ml_qk, ml_v, ml_o, ml_if, conv_w, conv_b, ifb, nw, batch, seq):
    nh = ML_HEADS
    col = lambda off: pl.BlockSpec((None, seq, LANES), lambda b, h: (b, 0, off + h))
    wcol = lambda rows, off: pl.BlockSpec((rows, LANES), lambda b, h: (0, off + h))
    return pl.pallas_call(
        functools.partial(_mlstm_kernel, seq=seq),
        grid=(batch, nh),
        in_specs=[col(0), col(nh), col(0), col(0),
                  pl.BlockSpec((None, seq, LANES), lambda b, h: (b, 0, 0)),
                  wcol(CONV_WIDTH, 0), wcol(CONV_WIDTH, nh), wcol(1, 0), wcol(1, nh),
                  pl.BlockSpec((1, LANES), lambda b, h: (0, 0)), wcol(1, 0)],
        out_specs=col(0),
        out_shape=jax.ShapeDtypeStruct((batch, seq, ML_WIDTH), BF16),
        scratch_shapes=[pltpu.VMEM((seq, LANES), BF16), pltpu.VMEM((seq, LANES), BF16),
                        pltpu.VMEM((seq, LANES), F32), pltpu.VMEM((seq, LANES), F32),
                        pltpu.VMEM((seq, LANES), F32), pltpu.VMEM((LANES, LANES), F32),
                        pltpu.VMEM((1, LANES), F32), pltpu.VMEM((1, LANES), F32)],
        compiler_params=_params(),
        name="mlstm",
    )(ml_qk, ml_qk, ml_v, ml_o, ml_if, conv_w, conv_w, conv_b, conv_b, ifb, nw)


def _merge_kernel(ysb, yml, g_ref, x_ref, wsb, wml, wout, nw_ref, wq, h_ref, hn_ref, q_ref):
    a = jnp.dot(ysb[...], wsb[...], preferred_element_type=F32)
    b = jnp.dot(yml[...], wml[...], preferred_element_type=F32)
    g = g_ref[...]
    merged = jax.nn.sigmoid(g[:, :D_MODEL]) * a + jax.nn.sigmoid(g[:, D_MODEL:]) * b
    h = x_ref[...] + jnp.dot(merged.astype(BF16), wout[...], preferred_element_type=F32)
    h_ref[...] = h
    hn = h * lax.rsqrt(jnp.mean(h * h, axis=-1, keepdims=True) + RMS_EPS) * nw_ref[...]
    hn_ref[...] = hn
    q_ref[...] = jnp.dot(hn.astype(BF16), wq[...], preferred_element_type=F32).astype(BF16)


def _merge(ysb, yml, gates, xt, wsb, wml, wout, nw, wq):
    t = xt.shape[0]
    tm = min(TM_PROJ, t)
    full = lambda a: pl.BlockSpec(a.shape, lambda i: (0, 0))
    tok = lambda n: pl.BlockSpec((tm, n), lambda i: (i, 0))
    nq = wq.shape[1]
    return pl.pallas_call(
        _merge_kernel,
        grid=(t // tm,),
        in_specs=[tok(SB_WIDTH), tok(ML_WIDTH), tok(2 * D_MODEL), tok(D_MODEL),
                  full(wsb), full(wml), full(wout), full(nw), full(wq)],
        out_specs=[tok(D_MODEL), tok(D_MODEL), tok(nq)],
        out_shape=[jax.ShapeDtypeStruct((t, D_MODEL), F32), jax.ShapeDtypeStruct((t, D_MODEL), F32),
                   jax.ShapeDtypeStruct((t, nq), BF16)],
        compiler_params=_params(),
        name="merge",
    )(ysb, yml, gates, xt, wsb, wml, wout, nw, wq)


_BIG = 3.0e38


def _extract_top(vals, payload, val_ref, pay_ref):
    for it in range(PEER_TOPK):
        m = jnp.max(vals, axis=0, keepdims=True)
        p = jnp.min(jnp.where(vals == m, payload, _BIG), axis=0, keepdims=True)
        val_ref[it:it + 1, :] = m
        pay_ref[it:it + 1, :] = p
        vals = jnp.where(payload == p, -jnp.inf, vals)


def _topk_kernel(q_ref, k1_ref, k2_ref, off_ref, gate_ref, v1, i1, v2, i2, tv, te, gall, *, tt):
    dn = (((1,), (1,)), ((), ()))
    key_iota = lax.broadcasted_iota(I32, (PEER_N_KEYS, tt), 0).astype(F32)
    a8 = lax.broadcasted_iota(I32, (SUBLANES, tt), 0)
    for h in range(PEER_HEADS):
        q1 = q_ref[:, 2 * h * PEER_HALF:(2 * h + 1) * PEER_HALF]
        q2 = q_ref[:, (2 * h + 1) * PEER_HALF:(2 * h + 2) * PEER_HALF]
        s1 = lax.dot_general(k1_ref[h], q1, dn, preferred_element_type=F32)
        s2 = lax.dot_general(k2_ref[h], q2, dn, preferred_element_type=F32)
        _extract_top(s1, key_iota, v1, i1)
        _extract_top(s2, key_iota, v2, i2)
        va, vb = v1[...], v2[...]
        ea, eb = i1[...] * float(PEER_N_KEYS), i2[...]
        lo8 = slice(0, SUBLANES)
        hi8 = slice(SUBLANES, 2 * SUBLANES)
        cv = [va[0:1] + vb, va[1:2] + vb[lo8], va[hi8] + vb[0:1]]
        ce = [ea[0:1] + eb, ea[1:2] + eb[lo8], ea[hi8] + eb[0:1]]
        for bcol, amax in ((0, 7), (1, 7), (2, 4), (3, 3), (4, 2)):
            ok = (a8 >= 2) & (a8 <= amax)
            cv.append(jnp.where(ok, va[lo8] + vb[bcol:bcol + 1], -jnp.inf))
            ce.append(ea[lo8] + eb[bcol:bcol + 1])
        _extract_top(jnp.concatenate(cv, axis=0), jnp.concatenate(ce, axis=0), tv, te)
        sc = tv[...]
        ex = jnp.exp(sc - jnp.max(sc, axis=0, keepdims=True))
        rows = slice(h * PEER_TOPK, (h + 1) * PEER_TOPK)
        gall[rows, :] = ex / jnp.sum(ex, axis=0, keepdims=True)
        off_ref[rows, :] = te[...].astype(I32) * WORD_ROWS
    gate_ref[...] = jnp.transpose(gall[...])


def _topk(q, k1, k2):
    t = q.shape[0]
    tt = min(TT_TOPK, t)
    kspec = pl.BlockSpec(k1.shape, lambda i: (0, 0, 0))
    small = lambda: pltpu.VMEM((PEER_TOPK, tt), F32)
    tok = pl.BlockSpec((tt, N_SEL), lambda i: (i, 0))
    return pl.pallas_call(
        functools.partial(_topk_kernel, tt=tt),
        grid=(t // tt,),
        in_specs=[pl.BlockSpec((tt, q.shape[1]), lambda i: (i, 0)), kspec, kspec],
        out_specs=[pl.BlockSpec((N_SEL, tt), lambda i: (0, i)), tok],
        out_shape=[jax.ShapeDtypeStruct((N_SEL, t), I32), jax.ShapeDtypeStruct((t, N_SEL), F32)],
        scratch_shapes=[small(), small(), small(), small(), small(), small(),
                        pltpu.VMEM((N_SEL, tt), F32)],
        compiler_params=_params(),
        name="topk",
    )(q, k1, k2)


def _off_copy(off_hbm, buf, sems, half_idx, s):
    return pltpu.make_async_copy(off_hbm.at[:, pl.ds(half_idx * PEER_HALF_TOK, PEER_HALF_TOK)], buf, sems.at[s])


def _run_halves(off_hbm, tab_hbm, off_a, off_b, tab_vmem, sems, run_half):
    i = pl.program_id(0)
    n = pl.num_programs(0)

    @pl.when(i == 0)
    def _():
        cp = pltpu.make_async_copy(tab_hbm, tab_vmem, sems.at[2])
        cp.start()
        _off_copy(off_hbm, off_a, sems, 0, 0).start()
        _off_copy(off_hbm, off_b, sems, 1, 1).start()
        cp.wait()

    for s, buf in enumerate((off_a, off_b)):
        _off_copy(off_hbm, buf, sems, 2 * i + s, s).wait()
        run_half(buf, s * PEER_HALF_TOK)

        @pl.when(i + 1 < n)
        def _():
            _off_copy(off_hbm, buf, sems, 2 * (i + 1) + s, s).start()


def _gather_token(off_smem, tab_vmem, col):
    rows = []
    for j in range(N_SEL):
        off = pl.multiple_of(off_smem.at[j][col], WORD_ROWS)
        rows.append(tab_vmem[pl.ds(off, WORD_ROWS), :])
    return pltpu.bitcast(jnp.concatenate(rows, axis=0), BF16)


def _slab_mask():
    s = lax.broadcasted_iota(I32, (SUBLANES, GATHER_ROWS), 0)
    n = lax.broadcasted_iota(I32, (SUBLANES, GATHER_ROWS), 1)
    return (n & (SUBLANES - 1)) == s


def _fold_matrix(shape, row_axis):
    n = lax.broadcasted_iota(I32, shape, row_axis)
    j = lax.broadcasted_iota(I32, shape, 1 - row_axis)
    return jnp.where(n // EXPERT_ROWS == j, 1.0, 0.0).astype(BF16)


def _merge_halves(a, b, m, k):
    r = pltpu.roll(b, k, axis=0)
    c = jnp.where(m, a, r)
    w = jnp.where(m, r, a)
    return c + pltpu.roll(w, SUBLANES - k, axis=0)


def _sublane_sums(xs, sub):
    m4, m2, m1 = sub < 4, (sub & 2) == 0, (sub & 1) == 0
    s = [_merge_halves(xs[a], xs[a + 4], m4, 4) for a in (0, 2, 1, 3)]
    return _merge_halves(_merge_halves(s[0], s[1], m2, 2), _merge_halves(s[2], s[3], m2, 2), m1, 1)


def _peer_u_kernel(off_hbm, x_ref, gate_ref, tab_hbm, w_ref, off_a, off_b, tab_vmem, cs_ref, sems):
    subw = lax.broadcasted_iota(I32, (SUBLANES, GATHER_ROWS), 0)
    mask = _slab_mask()
    dn = (((1,), (1,)), ((), ()))

    def run_half(off_smem, row_base):
        def group(g, carry):
            for oct_ in range(PEER_GROUP // SUBLANES):
                parts = []
                for tt in range(SUBLANES):
                    col = g * PEER_GROUP + oct_ * SUBLANES + tt
                    m = _gather_token(off_smem, tab_vmem, col)
                    r8 = pl.multiple_of((row_base + col) * SUBLANES, SUBLANES)
                    xh, xl = _split2(x_ref[pl.ds(r8, SUBLANES), :])
                    p = lax.dot_general(jnp.concatenate([xh, xl], axis=0), m, dn, preferred_element_type=F32)
                    parts.append(jnp.where(mask, p[:SUBLANES] + p[SUBLANES:], 0.0))
                r0 = pl.multiple_of(row_base + g * PEER_GROUP + oct_ * SUBLANES, SUBLANES)
                cs_ref[pl.ds(r0, SUBLANES), :] = _sublane_sums(parts, subw)
            return carry

        lax.fori_loop(0, PEER_HALF_TOK // PEER_GROUP, group, 0)

    _run_halves(off_hbm, tab_hbm, off_a, off_b, tab_vmem, sems, run_half)
    acts = _dot_exact_lhs(cs_ref[...], _fold_matrix((GATHER_ROWS, N_SEL), 0))
    w_ref[...] = gate_ref[...] * (0.5 * acts * (1.0 + lax.erf(acts * (2.0 ** -0.5))))


def _peer_scratch(tb):
    return [pltpu.SMEM((N_SEL, PEER_HALF_TOK), I32), pltpu.SMEM((N_SEL, PEER_HALF_TOK), I32),
            pltpu.VMEM((TAB_ROWS, LANES), I32), pltpu.VMEM((tb, GATHER_ROWS), F32),
            pltpu.SemaphoreType.DMA((3,))]


def _peer_u(off, x8, gate, tab):
    t = gate.shape[0]
    tb = TB_PEER
    tok = pl.BlockSpec((tb, N_SEL), lambda i: (i, 0))
    return pl.pallas_call(
        _peer_u_kernel,
        grid=(t // tb,),
        in_specs=[pl.BlockSpec(memory_space=pl.ANY),
                  pl.BlockSpec((tb * SUBLANES, LANES), lambda i: (i, 0)),
                  tok, pl.BlockSpec(memory_space=pl.ANY)],
        out_specs=tok,
        out_shape=jax.ShapeDtypeStruct((t, N_SEL), F32),
        scratch_shapes=_peer_scratch(tb),
        compiler_params=_params(),
        name="peer_u",
    )(off, x8, gate, tab)


def _peer_v_kernel(off_hbm, w_ref, h_ref, nw_ref, tab_hbm, out_ref, off_a, off_b, tab_vmem, wrep_ref, sems):
    mask = _slab_mask()
    wrep_ref[...] = _dot_exact_lhs(w_ref[...], _fold_matrix((N_SEL, GATHER_ROWS), 1))

    def run_half(off_smem, row_base):
        def group(g, carry):
            for oct_ in range(PEER_GROUP // SUBLANES):
                r0 = pl.multiple_of(row_base + g * PEER_GROUP + oct_ * SUBLANES, SUBLANES)
                wrep = wrep_ref[pl.ds(r0, SUBLANES), :]
                for tt in range(SUBLANES):
                    col = g * PEER_GROUP + oct_ * SUBLANES + tt
                    m = _gather_token(off_smem, tab_vmem, col)
                    ah, al = _split2(jnp.where(mask, wrep[tt:tt + 1, :], 0.0))
                    res = jnp.dot(jnp.concatenate([ah, al], axis=0), m, preferred_element_type=F32)
                    r8 = pl.multiple_of((row_base + col) * SUBLANES, SUBLANES)
                    y = h_ref[pl.ds(r8, SUBLANES), :] + res[:SUBLANES] + res[SUBLANES:]
                    ss = jnp.sum(jnp.sum(y * y, axis=1, keepdims=True), axis=0, keepdims=True)
                    out_ref[pl.ds(r8, SUBLANES), :] = (
                        y * lax.rsqrt(ss * (1.0 / D_MODEL) + RMS_EPS) * nw_ref[...])
            return carry

        lax.fori_loop(0, PEER_HALF_TOK // PEER_GROUP, group, 0)

    _run_halves(off_hbm, tab_hbm, off_a, off_b, tab_vmem, sems, run_half)


def _peer_v(off, w, h8, nw8, tab):
    t = w.shape[0]
    tb = TB_PEER
    tok = pl.BlockSpec((tb, N_SEL), lambda i: (i, 0))
    return pl.pallas_call(
        _peer_v_kernel,
        grid=(t // tb,),
        in_specs=[pl.BlockSpec(memory_space=pl.ANY), tok,
                  pl.BlockSpec((tb * SUBLANES, LANES), lambda i: (i, 0)),
                  pl.BlockSpec((SUBLANES, LANES), lambda i: (0, 0)),
                  pl.BlockSpec(memory_space=pl.ANY)],
        out_specs=pl.BlockSpec((tb * SUBLANES, LANES), lambda i: (i, 0)),
        out_shape=jax.ShapeDtypeStruct((t * SUBLANES, LANES), F32),
        scratch_shapes=_peer_scratch(tb),
        compiler_params=_params(),
        name="peer_v",
    )(off, w, h8, nw8, tab)


def _expert_table(w):
    rows = w.astype(BF16).reshape(TAB_ROWS, 2, LANES)
    return lax.bitcast_convert_type(jnp.swapaxes(rows, 1, 2), I32)


def kernel(x, norm_mix_w, w_in, conv_w, conv_b, b_igate, b_fgate, mlstm_norm_w, w_branch_sb, w_branch_ml, w_out, norm_ffn_w, peer_w_query, peer_keys1, peer_keys2, peer_u, peer_v, norm_final_w):
    batch, seq, d = x.shape
    t = batch * seq
    assert d == D_MODEL and w_in.shape[0] == 1
    assert seq % SBA_TQ == 0 and seq % ML_CHUNK == 0 and t % TM_PROJ == 0 and t % TT_TOPK == 0 and t % TB_PEER == 0
    xt = x.reshape(t, d)

    o_sb = 3 * SB_WIDTH
    o_qk = o_sb + 2 * ML_WIDTH
    o_v = o_qk + ML_WIDTH
    o_o = o_v + ML_WIDTH
    o_if = o_o + 2 * ML_HEADS
    w = w_in[0]
    wb = lambda a: a.astype(BF16)
    w_if = jnp.pad(w[:, o_o:o_if], ((0, 0), (0, LANES - 2 * ML_HEADS)))
    sb_qkv, ml_qk, ml_v, ml_o, gates, ml_if = _in_proj(
        xt, norm_mix_w[0].reshape(1, d), wb(w[:, :o_sb]), wb(w[:, o_sb:o_qk]), wb(w[:, o_qk:o_v]),
        wb(w[:, o_v:o_o]), wb(w[:, o_if:]), wb(w_if))

    y_sb = _sba(sb_qkv.reshape(batch, seq, 3 * SB_WIDTH), batch, seq)

    ifb = jnp.pad(jnp.concatenate([b_igate[0], b_fgate[0]]), (0, LANES - 2 * ML_HEADS)).reshape(1, LANES)
    y_ml = _mlstm(ml_qk.reshape(batch, seq, 2 * ML_WIDTH), ml_v.reshape(batch, seq, ML_WIDTH),
                  ml_o.reshape(batch, seq, ML_WIDTH), ml_if.reshape(batch, seq, LANES),
                  conv_w[0], conv_b[0].reshape(1, 2 * ML_WIDTH), ifb.astype(F32),
                  mlstm_norm_w[0].reshape(1, ML_WIDTH), batch, seq)

    h, hn, q = _merge(y_sb.reshape(t, SB_WIDTH), y_ml.reshape(t, ML_WIDTH), gates, xt,
                      wb(w_branch_sb[0]), wb(w_branch_ml[0]), wb(w_out[0]),
                      norm_ffn_w[0].reshape(1, d), wb(peer_w_query[0]))

    off, gate = _topk(q, wb(peer_keys1[0]), wb(peer_keys2[0]))

    wsel = _peer_u(off, hn.reshape(t * SUBLANES, LANES), gate, _expert_table(peer_u[0]))

    nw8 = norm_final_w.reshape(SUBLANES, LANES)
    out = _peer_v(off, wsel, h.reshape(t * SUBLANES, LANES), nw8, _expert_table(peer_v[0]))
    return out.reshape(batch, seq, d)
```

```python
import functools

import jax
import jax.numpy as jnp
from jax import lax
from jax.experimental import pallas as pl
from jax.experimental.pallas import tpu as pltpu

F32 = jnp.float32
BF16 = jnp.bfloat16
I32 = jnp.int32

D_MODEL = 1024
SB_HEADS = 8
SB_HEAD_DIM = 64
SB_WIDTH = SB_HEADS * SB_HEAD_DIM
ML_HEADS = 4
ML_HEAD_DIM = 128
ML_WIDTH = ML_HEADS * ML_HEAD_DIM
CONV_WIDTH = 4
PEER_HEADS = 8
PEER_N_KEYS = 128
PEER_N_EXPERTS = PEER_N_KEYS * PEER_N_KEYS
PEER_HALF = 128
PEER_TOPK = 16
N_SEL = PEER_HEADS * PEER_TOPK
RMS_EPS = 1e-6

LANES = 128
SUBLANES = 8
VMEM_LIMIT = 48 * 1024 * 1024

TM_PROJ = 256
SBA_TQ = 256
SBA_TK = 128
SBA_TK2 = 2 * SBA_TK
ML_CHUNK = 128
TT_TOPK = 256
PEER_HALF_TOK = 128
TB_PEER = 2 * PEER_HALF_TOK
PEER_GROUP = 16
EXPERT_ROWS = SUBLANES
WORD_ROWS = EXPERT_ROWS // 2
GATHER_ROWS = N_SEL * EXPERT_ROWS
TAB_ROWS = PEER_N_EXPERTS * WORD_ROWS
TABLE_BLOCK = 256


def _params():
    return pltpu.CompilerParams(dimension_semantics=None, vmem_limit_bytes=VMEM_LIMIT)


def _split3(a):
    hi = a.astype(BF16)
    r = a - hi.astype(F32)
    mid = r.astype(BF16)
    lo = (r - mid.astype(F32)).astype(BF16)
    return hi, mid, lo


def _split2(a):
    hi = a.astype(BF16)
    return hi, (a - hi.astype(F32)).astype(BF16)


def _dot_exact_rhs(a_bf16, b_f32):
    out = None
    for piece in _split3(b_f32):
        d = jnp.dot(a_bf16, piece, preferred_element_type=F32)
        out = d if out is None else out + d
    return out


def _dot_exact_lhs(a_f32, b_bf16):
    out = None
    for piece in _split3(a_f32):
        d = jnp.dot(piece, b_bf16, preferred_element_type=F32)
        out = d if out is None else out + d
    return out


def _inproj_kernel(x_ref, nw_ref, wsb, wqk, wv, wo, wg, wif, osb, oqk, ov, oo, og, oif):
    x = x_ref[...]
    ms = jnp.mean(x * x, axis=-1, keepdims=True)
    xb = (x * lax.rsqrt(ms + RMS_EPS) * nw_ref[...]).astype(BF16)
    osb[...] = jnp.dot(xb, wsb[...], preferred_element_type=F32).astype(BF16)
    oqk[...] = jnp.dot(xb, wqk[...], preferred_element_type=F32)
    ov[...] = jnp.dot(xb, wv[...], preferred_element_type=F32).astype(BF16)
    oo[...] = jnp.dot(xb, wo[...], preferred_element_type=F32)
    og[...] = jnp.dot(xb, wg[...], preferred_element_type=F32)
    oif[...] = jnp.dot(xb, wif[...], preferred_element_type=F32)


def _in_proj(xt, nw, wsb, wqk, wv, wo, wg, wif):
    t = xt.shape[0]
    tm = min(TM_PROJ, t)
    full = lambda a: pl.BlockSpec(a.shape, lambda i: (0, 0))
    tok = lambda n: pl.BlockSpec((tm, n), lambda i: (i, 0))
    ws = (wsb, wqk, wv, wo, wg, wif)
    outs = [(wsb.shape[1], BF16), (wqk.shape[1], F32), (wv.shape[1], BF16),
            (wo.shape[1], F32), (wg.shape[1], F32), (wif.shape[1], F32)]
    return pl.pallas_call(
        _inproj_kernel,
        grid=(t // tm,),
        in_specs=[tok(D_MODEL), full(nw)] + [full(w) for w in ws],
        out_specs=[tok(n) for n, _ in outs],
        out_shape=[jax.ShapeDtypeStruct((t, n), dt) for n, dt in outs],
        compiler_params=_params(),
        name="in_proj",
    )(xt, nw, *ws)


def _sba_step(qms, k_ref, v_ref, uu, carry_ref, acc_ref, kb2, row0, masked):
    c0 = pl.multiple_of(kb2 * SBA_TK2, SBA_TK2)
    kblk = k_ref[pl.ds(c0, SBA_TK2), :]
    vblk = v_ref[pl.ds(c0, SBA_TK2), :]
    if masked:
        rows = row0 + lax.broadcasted_iota(I32, (SBA_TQ, SBA_TK2), 0)
        cols = c0 + lax.broadcasted_iota(I32, (SBA_TQ, SBA_TK2), 1)
        causal = cols < rows
    for hh in range(2):
        z = lax.dot_general(qms[hh], kblk, (((1,), (1,)), ((), ())), preferred_element_type=F32)
        sp = jnp.maximum(z, 0.0) + jnp.log(1.0 + jnp.exp(-jnp.abs(z)))
        if masked:
            sp = jnp.where(causal, sp, 0.0)
        hi, lo = _split2(sp)
        su_lo = jnp.dot(jnp.concatenate([hi[:, :SBA_TK], lo[:, :SBA_TK]], axis=1), uu,
                        preferred_element_type=F32)
        su_hi = jnp.dot(jnp.concatenate([hi[:, SBA_TK:], lo[:, SBA_TK:]], axis=1), uu,
                        preferred_element_type=F32)
        carry = carry_ref[hh]
        later = jnp.concatenate([su_lo[:, :SBA_TK] + su_hi[:, SBA_TK:], su_hi[:, :SBA_TK]], axis=1)
        w = jnp.exp(z - sp - later - jnp.concatenate([carry, carry], axis=1))
        if masked:
            w = jnp.where(causal, w, 0.0)
        acc_ref[hh] += jnp.dot(w.astype(BF16), vblk, preferred_element_type=F32)
        carry_ref[hh] = carry + su_lo[:, SBA_TK:] + su_hi[:, SBA_TK:]


def _sba_kernel(q_ref, k_ref, v_ref, y_ref, carry_ref, acc_ref, *, seq):
    r = lax.broadcasted_iota(I32, (SBA_TK2, SBA_TK2), 0) % SBA_TK
    c = lax.broadcasted_iota(I32, (SBA_TK2, SBA_TK2), 1)
    uu = jnp.where((c >= SBA_TK) | (r > c), 1.0, 0.0).astype(BF16)
    lane = lax.broadcasted_iota(I32, (SBA_TQ, LANES), 1)
    assert SBA_TQ == SBA_TK2

    def qblock(qi, carry):
        row0 = pl.multiple_of(qi * SBA_TQ, SBA_TQ)
        qf = q_ref[pl.ds(row0, SBA_TQ), :] * jnp.asarray(SB_HEAD_DIM ** -0.5, BF16)
        zero = jnp.zeros_like(qf)
        qms = [jnp.where(lane < SB_HEAD_DIM, qf, zero), jnp.where(lane >= SB_HEAD_DIM, qf, zero)]
        carry_ref[...] = jnp.zeros_like(carry_ref)
        acc_ref[...] = jnp.zeros_like(acc_ref)
        _sba_step(qms, k_ref, v_ref, uu, carry_ref, acc_ref, qi, row0, True)

        def below(kk, c2):
            for d in range(2):
                _sba_step(qms, k_ref, v_ref, uu, carry_ref, acc_ref, qi - 1 - 2 * kk - d, row0, False)
            return c2

        lax.fori_loop(0, qi // 2, below, 0)

        @pl.when(qi % 2 == 1)
        def _():
            _sba_step(qms, k_ref, v_ref, uu, carry_ref, acc_ref, 0, row0, False)

        y = jnp.where(lane < SB_HEAD_DIM, acc_ref[0], acc_ref[1])
        y_ref[pl.ds(row0, SBA_TQ), :] = y.astype(BF16)
        return carry

    lax.fori_loop(0, seq // SBA_TQ, qblock, 0)


def _sba(sb_qkv, batch, seq):
    npair = SB_WIDTH // LANES
    blk = lambda off: pl.BlockSpec((None, seq, LANES), lambda b, p: (b, 0, off + p))
    return pl.pallas_call(
        functools.partial(_sba_kernel, seq=seq),
        grid=(batch, npair),
        in_specs=[blk(0), blk(npair), blk(2 * npair)],
        out_specs=pl.BlockSpec((None, seq, LANES), lambda b, p: (b, 0, p)),
        out_shape=jax.ShapeDtypeStruct((batch, seq, SB_WIDTH), BF16),
        scratch_shapes=[pltpu.VMEM((2, SBA_TQ, SBA_TK), F32), pltpu.VMEM((2, SBA_TQ, LANES), F32)],
        compiler_params=_params(),
        name="sba",
    )(sb_qkv, sb_qkv, sb_qkv)


def _conv_silu(x, w, b, row):
    y = x * w[CONV_WIDTH - 1:CONV_WIDTH, :] + b
    for k in range(1, CONV_WIDTH):
        shifted = jnp.where(row >= k, pltpu.roll(x, k, axis=0), 0.0)
        y = y + shifted * w[CONV_WIDTH - 1 - k:CONV_WIDTH - k, :]
    return y * jax.nn.sigmoid(y)


def _log_sigmoid(x):
    return -(jnp.maximum(-x, 0.0) + jnp.log1p(jnp.exp(-jnp.abs(x))))


def _mlstm_kernel(qk_ref, v_ref, o_ref, if_ref, cw_ref, cb_ref, ifb_ref, nw_ref, y_ref,
                  qs_ref, ks_ref, c_ref, n_ref, m_ref, *, seq):
    L = ML_CHUNK
    row = lax.broadcasted_iota(I32, (seq, LANES), 0)

    def conv_head(hd, carry):
        cq = pl.multiple_of(hd * LANES, LANES)
        ck = pl.multiple_of(ML_WIDTH + hd * LANES, LANES)
        q = _conv_silu(qk_ref[:, pl.ds(cq, LANES)], cw_ref[:, pl.ds(cq, LANES)], cb_ref[:, pl.ds(cq, LANES)], row)
        k = _conv_silu(qk_ref[:, pl.ds(ck, LANES)], cw_ref[:, pl.ds(ck, LANES)], cb_ref[:, pl.ds(ck, LANES)], row)
        qs_ref[hd] = q.astype(BF16)
        ks_ref[hd] = (k * (ML_HEAD_DIM ** -0.5)).astype(BF16)
        return carry

    lax.fori_loop(0, ML_HEADS, conv_head, 0)

    src = lax.broadcasted_iota(I32, (LANES, LANES), 0)
    rr = lax.broadcasted_iota(I32, (L, L), 0)
    cc = lax.broadcasted_iota(I32, (L, L), 1)
    tril = cc <= rr
    tril_b = jnp.where(tril, 1.0, 0.0).astype(BF16)
    eye = jnp.where(cc == rr, 1.0, 0.0)
    ones_b = jnp.ones((L, L), BF16)

    c_ref[...] = jnp.zeros_like(c_ref)
    n_ref[...] = jnp.zeros_like(n_ref)
    m_ref[...] = jnp.zeros_like(m_ref)

    def chunk(ci, carry):
        r0 = pl.multiple_of(ci * L, L)
        gates = if_ref[pl.ds(r0, L), :] + ifb_ref[...]
        for hd in range(ML_HEADS):
            lanes = slice(hd * LANES, (hd + 1) * LANES)
            ic = _dot_exact_lhs(gates, jnp.where(src == hd, 1.0, 0.0).astype(BF16))
            lf = _log_sigmoid(_dot_exact_lhs(gates, jnp.where(src == hd + ML_HEADS, 1.0, 0.0).astype(BF16)))
            qc = qs_ref[hd, pl.ds(r0, L), :]
            kc = ks_ref[hd, pl.ds(r0, L), :]
            vc = v_ref[pl.ds(r0, L), lanes]
            m = m_ref[hd]
            n = n_ref[hd]
            cmat = c_ref[hd]
            b = _dot_exact_rhs(tril_b, lf)
            drow = _dot_exact_rhs(ones_b, eye * (ic - b))
            dmat = jnp.where(tril, b + drow, -jnp.inf)
            inter = b + m
            m_t = jnp.maximum(inter, jnp.max(dmat, axis=-1, keepdims=True))
            w_intra = jnp.exp(dmat - m_t)
            w_inter = jnp.exp(inter - m_t)
            s = lax.dot_general(qc, kc, (((1,), (1,)), ((), ())), preferred_element_type=F32) * w_intra
            num = jnp.dot(s.astype(BF16), vc, preferred_element_type=F32)
            num = num + w_inter * jnp.dot(qc, cmat.astype(BF16), preferred_element_type=F32)
            qn = jnp.sum(qc.astype(F32) * n, axis=-1, keepdims=True)
            den = jnp.sum(s, axis=-1, keepdims=True) + w_inter * qn
            h = num / jnp.maximum(jnp.abs(den), jnp.exp(-m_t))
            hn = h * lax.rsqrt(jnp.mean(h * h, axis=-1, keepdims=True) + RMS_EPS) * nw_ref[:, lanes]
            y_ref[pl.ds(r0, L), lanes] = (jax.nn.sigmoid(o_ref[pl.ds(r0, L), lanes]) * hn).astype(BF16)
            b_last = b[L - 1:L, :]
            g = b_last - b + ic
            m_new = jnp.maximum(b_last + m, jnp.max(g, axis=0, keepdims=True))
            decay = jnp.exp(b_last + m - m_new)
            kw = kc.astype(F32) * jnp.exp(g - m_new)
            c_ref[hd] = decay * cmat + lax.dot_general(
                kw.astype(BF16), vc, (((0,), (0,)), ((), ())), preferred_element_type=F32)
            n_ref[hd] = decay * n + jnp.sum(kw, axis=0, keepdims=True)
            m_ref[hd] = m_new
        return carry

    lax.fori_loop(0, seq // L, chunk, 0)


def _mlstm(ml_qk, ml_v, ml_o, ml_if, conv_w, conv_b, ifb, nw, batch, seq):
    rows = lambda n: pl.BlockSpec((None, seq, n), lambda b: (b, 0, 0))
    full = lambda a: pl.BlockSpec(a.shape, lambda b: (0, 0))
    return pl.pallas_call(
        functools.partial(_mlstm_kernel, seq=seq),
        grid=(batch,),
        in_specs=[rows(2 * ML_WIDTH), rows(ML_WIDTH), rows(ML_WIDTH), rows(LANES),
                  full(conv_w), full(conv_b), full(ifb), full(nw)],
        out_specs=rows(ML_WIDTH),
        out_shape=jax.ShapeDtypeStruct((batch, seq, ML_WIDTH), BF16),
        scratch_shapes=[pltpu.VMEM((ML_HEADS, seq, LANES), BF16), pltpu.VMEM((ML_HEADS, seq, LANES), BF16),
                        pltpu.VMEM((ML_HEADS, LANES, LANES), F32),
                        pltpu.VMEM((ML_HEADS, 1, LANES), F32), pltpu.VMEM((ML_HEADS, 1, LANES), F32)],
        compiler_params=_params(),
        name="mlstm",
    )(ml_qk, ml_v, ml_o, ml_if, conv_w, conv_b, ifb, nw)


def _to_slabs(a):
    return pltpu.einshape("t(rl)->(tr)l", a, r=SUBLANES)


def _merge_kernel(ysb, yml, g_ref, x_ref, wsb, wml, wout, nw_ref, wq, h_ref, hn_ref, q_ref):
    a = jnp.dot(ysb[...], wsb[...], preferred_element_type=F32)
    b = jnp.dot(yml[...], wml[...], preferred_element_type=F32)
    g = g_ref[...]
    merged = jax.nn.sigmoid(g[:, :D_MODEL]) * a + jax.nn.sigmoid(g[:, D_MODEL:]) * b
    h = x_ref[...] + jnp.dot(merged.astype(BF16), wout[...], preferred_element_type=F32)
    h_ref[...] = _to_slabs(h)
    hn = h * lax.rsqrt(jnp.mean(h * h, axis=-1, keepdims=True) + RMS_EPS) * nw_ref[...]
    hn_ref[...] = _to_slabs(hn)
    q_ref[...] = jnp.dot(hn.astype(BF16), wq[...], preferred_element_type=F32).astype(BF16)


def _merge(ysb, yml, gates, xt, wsb, wml, wout, nw, wq):
    t = xt.shape[0]
    tm = min(TM_PROJ, t)
    full = lambda a: pl.BlockSpec(a.shape, lambda i: (0, 0))
    tok = lambda n: pl.BlockSpec((tm, n), lambda i: (i, 0))
    slab = pl.BlockSpec((tm * SUBLANES, LANES), lambda i: (i, 0))
    nq = wq.shape[1]
    return pl.pallas_call(
        _merge_kernel,
        grid=(t // tm,),
        in_specs=[tok(SB_WIDTH), tok(ML_WIDTH), tok(2 * D_MODEL), tok(D_MODEL),
                  full(wsb), full(wml), full(wout), full(nw), full(wq)],
        out_specs=[slab, slab, tok(nq)],
        out_shape=[jax.ShapeDtypeStruct((t * SUBLANES, LANES), F32),
                   jax.ShapeDtypeStruct((t * SUBLANES, LANES), F32),
                   jax.ShapeDtypeStruct((t, nq), BF16)],
        compiler_params=_params(),
        name="merge",
    )(ysb, yml, gates, xt, wsb, wml, wout, nw, wq)


_BIG = 3.0e38


def _extract_top(vals, payload, val_ref, pay_ref):
    for it in range(PEER_TOPK):
        m = jnp.max(vals, axis=0, keepdims=True)
        p = jnp.min(jnp.where(vals == m, payload, _BIG), axis=0, keepdims=True)
        val_ref[it:it + 1, :] = m
        pay_ref[it:it + 1, :] = p
        vals = jnp.where(payload == p, -jnp.inf, vals)


def _topk_kernel(q_ref, k1_ref, k2_ref, off_ref, gate_ref, v1, i1, v2, i2, tv, te, gall, *, tt):
    dn = (((1,), (1,)), ((), ()))
    key_iota = lax.broadcasted_iota(I32, (PEER_N_KEYS, tt), 0).astype(F32)
    a8 = lax.broadcasted_iota(I32, (SUBLANES, tt), 0)
    for h in range(PEER_HEADS):
        q1 = q_ref[:, 2 * h * PEER_HALF:(2 * h + 1) * PEER_HALF]
        q2 = q_ref[:, (2 * h + 1) * PEER_HALF:(2 * h + 2) * PEER_HALF]
        s1 = lax.dot_general(k1_ref[h], q1, dn, preferred_element_type=F32)
        s2 = lax.dot_general(k2_ref[h], q2, dn, preferred_element_type=F32)
        _extract_top(s1, key_iota, v1, i1)
        _extract_top(s2, key_iota, v2, i2)
        va, vb = v1[...], v2[...]
        ea, eb = i1[...] * float(PEER_N_KEYS), i2[...]
        lo8 = slice(0, SUBLANES)
        hi8 = slice(SUBLANES, 2 * SUBLANES)
        cv = [va[0:1] + vb, va[1:2] + vb[lo8], va[hi8] + vb[0:1]]
        ce = [ea[0:1] + eb, ea[1:2] + eb[lo8], ea[hi8] + eb[0:1]]
        for bcol, amax in ((0, 7), (1, 7), (2, 4), (3, 3), (4, 2)):
            ok = (a8 >= 2) & (a8 <= amax)
            cv.append(jnp.where(ok, va[lo8] + vb[bcol:bcol + 1], -jnp.inf))
            ce.append(ea[lo8] + eb[bcol:bcol + 1])
        _extract_top(jnp.concatenate(cv, axis=0), jnp.concatenate(ce, axis=0), tv, te)
        sc = tv[...]
        ex = jnp.exp(sc - jnp.max(sc, axis=0, keepdims=True))
        rows = slice(h * PEER_TOPK, (h + 1) * PEER_TOPK)
        gall[rows, :] = ex / jnp.sum(ex, axis=0, keepdims=True)
        off_ref[rows, :] = te[...].astype(I32) * WORD_ROWS
    gate_ref[...] = jnp.transpose(gall[...])


def _topk(q, k1, k2):
    t = q.shape[0]
    tt = min(TT_TOPK, t)
    kspec = pl.BlockSpec(k1.shape, lambda i: (0, 0, 0))
    small = lambda: pltpu.VMEM((PEER_TOPK, tt), F32)
    tok = pl.BlockSpec((tt, N_SEL), lambda i: (i, 0))
    return pl.pallas_call(
        functools.partial(_topk_kernel, tt=tt),
        grid=(t // tt,),
        in_specs=[pl.BlockSpec((tt, q.shape[1]), lambda i: (i, 0)), kspec, kspec],
        out_specs=[pl.BlockSpec((N_SEL, tt), lambda i: (0, i)), tok],
        out_shape=[jax.ShapeDtypeStruct((N_SEL, t), I32), jax.ShapeDtypeStruct((t, N_SEL), F32)],
        scratch_shapes=[small(), small(), small(), small(), small(), small(),
                        pltpu.VMEM((N_SEL, tt), F32)],
        compiler_params=_params(),
        name="topk",
    )(q, k1, k2)


def _off_copy(off_hbm, buf, sems, half_idx, s):
    return pltpu.make_async_copy(off_hbm.at[:, pl.ds(half_idx * PEER_HALF_TOK, PEER_HALF_TOK)], buf, sems.at[s])


def _run_halves(off_hbm, tab_hbm, off_a, off_b, tab_vmem, sems, run_half):
    i = pl.program_id(0)
    n = pl.num_programs(0)

    @pl.when(i == 0)
    def _():
        cp = pltpu.make_async_copy(tab_hbm, tab_vmem, sems.at[2])
        cp.start()
        _off_copy(off_hbm, off_a, sems, 0, 0).start()
        _off_copy(off_hbm, off_b, sems, 1, 1).start()
        cp.wait()

    for s, buf in enumerate((off_a, off_b)):
        _off_copy(off_hbm, buf, sems, 2 * i + s, s).wait()
        run_half(buf, s * PEER_HALF_TOK)

        @pl.when(i + 1 < n)
        def _():
            _off_copy(off_hbm, buf, sems, 2 * (i + 1) + s, s).start()


def _gather_token(off_smem, tab_vmem, col):
    rows = []
    for j in range(N_SEL):
        off = pl.multiple_of(off_smem.at[j][col], WORD_ROWS)
        rows.append(tab_vmem[pl.ds(off, WORD_ROWS), :])
    return pltpu.bitcast(jnp.concatenate(rows, axis=0), BF16)


def _slab_mask():
    s = lax.broadcasted_iota(I32, (SUBLANES, GATHER_ROWS), 0)
    n = lax.broadcasted_iota(I32, (SUBLANES, GATHER_ROWS), 1)
    return (n & (SUBLANES - 1)) == s


def _fold_matrix(shape, row_axis):
    n = lax.broadcasted_iota(I32, shape, row_axis)
    j = lax.broadcasted_iota(I32, shape, 1 - row_axis)
    return jnp.where(n // EXPERT_ROWS == j, 1.0, 0.0).astype(BF16)


def _merge_halves(a, b, m, k):
    r = pltpu.roll(b, k, axis=0)
    c = jnp.where(m, a, r)
    w = jnp.where(m, r, a)
    return c + pltpu.roll(w, SUBLANES - k, axis=0)


def _sublane_sums(xs, sub):
    m4, m2, m1 = sub < 4, (sub & 2) == 0, (sub & 1) == 0
    s = [_merge_halves(xs[a], xs[a + 4], m4, 4) for a in (0, 2, 1, 3)]
    return _merge_halves(_merge_halves(s[0], s[1], m2, 2), _merge_halves(s[2], s[3], m2, 2), m1, 1)


def _peer_u_kernel(off_hbm, x_ref, gate_ref, tab_hbm, w_ref, off_a, off_b, tab_vmem, cs_ref, sems):
    subw = lax.broadcasted_iota(I32, (SUBLANES, GATHER_ROWS), 0)
    mask = _slab_mask()
    dn = (((1,), (1,)), ((), ()))

    def run_half(off_smem, row_base):
        def group(g, carry):
            for oct_ in range(PEER_GROUP // SUBLANES):
                parts = []
                for tt in range(SUBLANES):
                    col = g * PEER_GROUP + oct_ * SUBLANES + tt
                    m = _gather_token(off_smem, tab_vmem, col)
                    r8 = pl.multiple_of((row_base + col) * SUBLANES, SUBLANES)
                    xh, xl = _split2(x_ref[pl.ds(r8, SUBLANES), :])
                    p = lax.dot_general(jnp.concatenate([xh, xl], axis=0), m, dn, preferred_element_type=F32)
                    parts.append(jnp.where(mask, p[:SUBLANES] + p[SUBLANES:], 0.0))
                r0 = pl.multiple_of(row_base + g * PEER_GROUP + oct_ * SUBLANES, SUBLANES)
                cs_ref[pl.ds(r0, SUBLANES), :] = _sublane_sums(parts, subw)
            return carry

        lax.fori_loop(0, PEER_HALF_TOK // PEER_GROUP, group, 0)

    _run_halves(off_hbm, tab_hbm, off_a, off_b, tab_vmem, sems, run_half)
    acts = _dot_exact_lhs(cs_ref[...], _fold_matrix((GATHER_ROWS, N_SEL), 0))
    w_ref[...] = gate_ref[...] * (0.5 * acts * (1.0 + lax.erf(acts * (2.0 ** -0.5))))


def _peer_scratch(tb, *extra):
    return [pltpu.SMEM((N_SEL, PEER_HALF_TOK), I32), pltpu.SMEM((N_SEL, PEER_HALF_TOK), I32),
            pltpu.VMEM((TAB_ROWS, LANES), I32), pltpu.VMEM((tb, GATHER_ROWS), F32), *extra,
            pltpu.SemaphoreType.DMA((3,))]


def _peer_u(off, x8, gate, tab):
    t = gate.shape[0]
    tb = TB_PEER
    tok = pl.BlockSpec((tb, N_SEL), lambda i: (i, 0))
    return pl.pallas_call(
        _peer_u_kernel,
        grid=(t // tb,),
        in_specs=[pl.BlockSpec(memory_space=pl.ANY),
                  pl.BlockSpec((tb * SUBLANES, LANES), lambda i: (i, 0)),
                  tok, pl.BlockSpec(memory_space=pl.ANY)],
        out_specs=tok,
        out_shape=jax.ShapeDtypeStruct((t, N_SEL), F32),
        scratch_shapes=_peer_scratch(tb),
        compiler_params=_params(),
        name="peer_u",
    )(off, x8, gate, tab)


def _peer_v_kernel(off_hbm, w_ref, h_ref, nw_ref, tab_hbm, out_ref, off_a, off_b, tab_vmem, wrep_ref, y_ref, sems):
    mask = _slab_mask()
    wrep_ref[...] = _dot_exact_lhs(w_ref[...], _fold_matrix((N_SEL, GATHER_ROWS), 1))

    def run_half(off_smem, row_base):
        def group(g, carry):
            for oct_ in range(PEER_GROUP // SUBLANES):
                r0 = pl.multiple_of(row_base + g * PEER_GROUP + oct_ * SUBLANES, SUBLANES)
                wrep = wrep_ref[pl.ds(r0, SUBLANES), :]
                for tt in range(SUBLANES):
                    col = g * PEER_GROUP + oct_ * SUBLANES + tt
                    m = _gather_token(off_smem, tab_vmem, col)
                    ah, al = _split2(jnp.where(mask, wrep[tt:tt + 1, :], 0.0))
                    res = jnp.dot(jnp.concatenate([ah, al], axis=0), m, preferred_element_type=F32)
                    r8 = pl.multiple_of((row_base + col) * SUBLANES, SUBLANES)
                    y = h_ref[pl.ds(r8, SUBLANES), :] + res[:SUBLANES] + res[SUBLANES:]
                    ss = jnp.sum(jnp.sum(y * y, axis=1, keepdims=True), axis=0, keepdims=True)
                    y_ref[pl.ds(r8, SUBLANES), :] = (
                        y * lax.rsqrt(ss * (1.0 / D_MODEL) + RMS_EPS) * nw_ref[...])
            return carry

        lax.fori_loop(0, PEER_HALF_TOK // PEER_GROUP, group, 0)

    _run_halves(off_hbm, tab_hbm, off_a, off_b, tab_vmem, sems, run_half)
    out_ref[...] = pltpu.einshape("(tr)l->t(rl)", y_ref[...], r=SUBLANES)


def _peer_v(off, w, h8, nw8, tab):
    t = w.shape[0]
    tb = TB_PEER
    tok = pl.BlockSpec((tb, N_SEL), lambda i: (i, 0))
    return pl.pallas_call(
        _peer_v_kernel,
        grid=(t // tb,),
        in_specs=[pl.BlockSpec(memory_space=pl.ANY), tok,
                  pl.BlockSpec((tb * SUBLANES, LANES), lambda i: (i, 0)),
                  pl.BlockSpec((SUBLANES, LANES), lambda i: (0, 0)),
                  pl.BlockSpec(memory_space=pl.ANY)],
        out_specs=pl.BlockSpec((tb, D_MODEL), lambda i: (i, 0)),
        out_shape=jax.ShapeDtypeStruct((t, D_MODEL), F32),
        scratch_shapes=_peer_scratch(tb, pltpu.VMEM((tb * SUBLANES, LANES), F32)),
        compiler_params=_params(),
        name="peer_v",
    )(off, w, h8, nw8, tab)


def _table_kernel(w_ref, o_ref):
    rows = pltpu.einshape("e(rl)->(er)l", w_ref[...].astype(BF16), r=EXPERT_ROWS)
    o_ref[...] = pltpu.bitcast(rows, I32)


def _expert_table(w):
    n = w.shape[0]
    eb = TABLE_BLOCK
    return pl.pallas_call(
        _table_kernel,
        grid=(n // eb,),
        in_specs=[pl.BlockSpec((eb, D_MODEL), lambda i: (i, 0))],
        out_specs=pl.BlockSpec((eb * WORD_ROWS, LANES), lambda i: (i, 0)),
        out_shape=jax.ShapeDtypeStruct((n * WORD_ROWS, LANES), I32),
        compiler_params=_params(),
        name="expert_table",
    )(w)


def kernel(x, norm_mix_w, w_in, conv_w, conv_b, b_igate, b_fgate, mlstm_norm_w, w_branch_sb, w_branch_ml, w_out, norm_ffn_w, peer_w_query, peer_keys1, peer_keys2, peer_u, peer_v, norm_final_w):
    batch, seq, d = x.shape
    t = batch * seq
    assert d == D_MODEL and w_in.shape[0] == 1
    assert seq % SBA_TQ == 0 and seq % ML_CHUNK == 0 and t % TM_PROJ == 0 and t % TT_TOPK == 0 and t % TB_PEER == 0
    xt = x.reshape(t, d)

    o_sb = 3 * SB_WIDTH
    o_qk = o_sb + 2 * ML_WIDTH
    o_v = o_qk + ML_WIDTH
    o_o = o_v + ML_WIDTH
    o_if = o_o + 2 * ML_HEADS
    w = w_in[0]
    wb = lambda a: a.astype(BF16)
    w_if = jnp.pad(w[:, o_o:o_if], ((0, 0), (0, LANES - 2 * ML_HEADS)))
    sb_qkv, ml_qk, ml_v, ml_o, gates, ml_if = _in_proj(
        xt, norm_mix_w[0].reshape(1, d), wb(w[:, :o_sb]), wb(w[:, o_sb:o_qk]), wb(w[:, o_qk:o_v]),
        wb(w[:, o_v:o_o]), wb(w[:, o_if:]), wb(w_if))

    y_sb = _sba(sb_qkv.reshape(batch, seq, 3 * SB_WIDTH), batch, seq)

    ifb = jnp.pad(jnp.concatenate([b_igate[0], b_fgate[0]]), (0, LANES - 2 * ML_HEADS)).reshape(1, LANES)
    y_ml = _mlstm(ml_qk.reshape(batch, seq, 2 * ML_WIDTH), ml_v.reshape(batch, seq, ML_WIDTH),
                  ml_o.reshape(batch, seq, ML_WIDTH), ml_if.reshape(batch, seq, LANES),
                  conv_w[0], conv_b[0].reshape(1, 2 * ML_WIDTH), ifb.astype(F32),
                  mlstm_norm_w[0].reshape(1, ML_WIDTH), batch, seq)

    h, hn, q = _merge(y_sb.reshape(t, SB_WIDTH), y_ml.reshape(t, ML_WIDTH), gates, xt,
                      wb(w_branch_sb[0]), wb(w_branch_ml[0]), wb(w_out[0]),
                      norm_ffn_w[0].reshape(1, d), wb(peer_w_query[0]))

    off, gate = _topk(q, wb(peer_keys1[0]), wb(peer_keys2[0]))

    wsel = _peer_u(off, hn, gate, _expert_table(peer_u[0]))

    nw8 = norm_final_w.reshape(SUBLANES, LANES)
    out = _peer_v(off, wsel, h, nw8, _expert_table(peer_v[0]))
    return out.reshape(batch, seq, d)
```

```python
import functools

import jax
import jax.numpy as jnp
from jax import lax
from jax.experimental import pallas as pl
from jax.experimental.pallas import tpu as pltpu

F32 = jnp.float32
BF16 = jnp.bfloat16
I32 = jnp.int32

D_MODEL = 1024
SB_HEADS = 8
SB_HEAD_DIM = 64
SB_WIDTH = SB_HEADS * SB_HEAD_DIM
ML_HEADS = 4
ML_HEAD_DIM = 128
ML_WIDTH = ML_HEADS * ML_HEAD_DIM
CONV_WIDTH = 4
PEER_HEADS = 8
PEER_N_KEYS = 128
PEER_N_EXPERTS = PEER_N_KEYS * PEER_N_KEYS
PEER_HALF = 128
PEER_TOPK = 16
N_SEL = PEER_HEADS * PEER_TOPK
RMS_EPS = 1e-6

LANES = 128
SUBLANES = 8
VMEM_LIMIT = 48 * 1024 * 1024

TM_PROJ = 256
SBA_TQ = 256
SBA_TK = 128
SBA_TK2 = 2 * SBA_TK
ML_CHUNK = 128
TT_TOPK = 256
PEER_HALF_TOK = 128
TB_PEER = 2 * PEER_HALF_TOK
PEER_GROUP = 32
EXPERT_ROWS = SUBLANES
WORD_ROWS = EXPERT_ROWS // 2
GATHER_ROWS = N_SEL * EXPERT_ROWS
TAB_ROWS = PEER_N_EXPERTS * WORD_ROWS
TABLE_BLOCK = 256


def _params():
    return pltpu.CompilerParams(dimension_semantics=None, vmem_limit_bytes=VMEM_LIMIT)


def _split3(a):
    hi = a.astype(BF16)
    r = a - hi.astype(F32)
    mid = r.astype(BF16)
    lo = (r - mid.astype(F32)).astype(BF16)
    return hi, mid, lo


def _split2(a):
    hi = a.astype(BF16)
    return hi, (a - hi.astype(F32)).astype(BF16)


def _dot_exact_rhs(a_bf16, b_f32):
    out = None
    for piece in _split3(b_f32):
        d = jnp.dot(a_bf16, piece, preferred_element_type=F32)
        out = d if out is None else out + d
    return out


def _dot_exact_lhs(a_f32, b_bf16):
    out = None
    for piece in _split3(a_f32):
        d = jnp.dot(piece, b_bf16, preferred_element_type=F32)
        out = d if out is None else out + d
    return out


def _inproj_kernel(x_ref, nw_ref, wsb, wqk, wv, wo, wg, wif, osb, oqk, ov, oo, og, oif):
    x = x_ref[...]
    ms = jnp.mean(x * x, axis=-1, keepdims=True)
    xb = (x * lax.rsqrt(ms + RMS_EPS) * nw_ref[...]).astype(BF16)
    osb[...] = jnp.dot(xb, wsb[...], preferred_element_type=F32).astype(BF16)
    oqk[...] = jnp.dot(xb, wqk[...], preferred_element_type=F32)
    ov[...] = jnp.dot(xb, wv[...], preferred_element_type=F32).astype(BF16)
    oo[...] = jnp.dot(xb, wo[...], preferred_element_type=F32)
    og[...] = jnp.dot(xb, wg[...], preferred_element_type=F32)
    oif[...] = jnp.dot(xb, wif[...], preferred_element_type=F32)


def _in_proj(xt, nw, wsb, wqk, wv, wo, wg, wif):
    t = xt.shape[0]
    tm = min(TM_PROJ, t)
    full = lambda a: pl.BlockSpec(a.shape, lambda i: (0, 0))
    tok = lambda n: pl.BlockSpec((tm, n), lambda i: (i, 0))
    ws = (wsb, wqk, wv, wo, wg, wif)
    outs = [(wsb.shape[1], BF16), (wqk.shape[1], F32), (wv.shape[1], BF16),
            (wo.shape[1], F32), (wg.shape[1], F32), (wif.shape[1], F32)]
    return pl.pallas_call(
        _inproj_kernel,
        grid=(t // tm,),
        in_specs=[tok(D_MODEL), full(nw)] + [full(w) for w in ws],
        out_specs=[tok(n) for n, _ in outs],
        out_shape=[jax.ShapeDtypeStruct((t, n), dt) for n, dt in outs],
        compiler_params=_params(),
        name="in_proj",
    )(xt, nw, *ws)


def _sba_step(qms, k_ref, v_ref, uu, carry_ref, acc_ref, kb2, row0, masked):
    c0 = pl.multiple_of(kb2 * SBA_TK2, SBA_TK2)
    kblk = k_ref[pl.ds(c0, SBA_TK2), :]
    vblk = v_ref[pl.ds(c0, SBA_TK2), :]
    if masked:
        rows = row0 + lax.broadcasted_iota(I32, (SBA_TQ, SBA_TK2), 0)
        cols = c0 + lax.broadcasted_iota(I32, (SBA_TQ, SBA_TK2), 1)
        causal = cols < rows
    for hh in range(2):
        z = lax.dot_general(qms[hh], kblk, (((1,), (1,)), ((), ())), preferred_element_type=F32)
        sp = jnp.maximum(z, 0.0) + jnp.log(1.0 + jnp.exp(-jnp.abs(z)))
        if masked:
            sp = jnp.where(causal, sp, 0.0)
        hi, lo = _split2(sp)
        su_lo = jnp.dot(jnp.concatenate([hi[:, :SBA_TK], lo[:, :SBA_TK]], axis=1), uu,
                        preferred_element_type=F32)
        su_hi = jnp.dot(jnp.concatenate([hi[:, SBA_TK:], lo[:, SBA_TK:]], axis=1), uu,
                        preferred_element_type=F32)
        carry = carry_ref[hh]
        later = jnp.concatenate([su_lo[:, :SBA_TK] + su_hi[:, SBA_TK:], su_hi[:, :SBA_TK]], axis=1)
        w = jnp.exp(z - sp - later - jnp.concatenate([carry, carry], axis=1))
        if masked:
            w = jnp.where(causal, w, 0.0)
        acc_ref[hh] += jnp.dot(w.astype(BF16), vblk, preferred_element_type=F32)
        carry_ref[hh] = carry + su_lo[:, SBA_TK:] + su_hi[:, SBA_TK:]


def _sba_kernel(q_ref, k_ref, v_ref, y_ref, carry_ref, acc_ref, *, seq):
    r = lax.broadcasted_iota(I32, (SBA_TK2, SBA_TK2), 0) % SBA_TK
    c = lax.broadcasted_iota(I32, (SBA_TK2, SBA_TK2), 1)
    uu = jnp.where((c >= SBA_TK) | (r > c), 1.0, 0.0).astype(BF16)
    lane = lax.broadcasted_iota(I32, (SBA_TQ, LANES), 1)
    assert SBA_TQ == SBA_TK2

    def qblock(qi, carry):
        row0 = pl.multiple_of(qi * SBA_TQ, SBA_TQ)
        qf = q_ref[pl.ds(row0, SBA_TQ), :] * jnp.asarray(SB_HEAD_DIM ** -0.5, BF16)
        zero = jnp.zeros_like(qf)
        qms = [jnp.where(lane < SB_HEAD_DIM, qf, zero), jnp.where(lane >= SB_HEAD_DIM, qf, zero)]
        carry_ref[...] = jnp.zeros_like(carry_ref)
        acc_ref[...] = jnp.zeros_like(acc_ref)
        _sba_step(qms, k_ref, v_ref, uu, carry_ref, acc_ref, qi, row0, True)

        def below(kk, c2):
            for d in range(2):
                _sba_step(qms, k_ref, v_ref, uu, carry_ref, acc_ref, qi - 1 - 2 * kk - d, row0, False)
            return c2

        lax.fori_loop(0, qi // 2, below, 0)

        @pl.when(qi % 2 == 1)
        def _():
            _sba_step(qms, k_ref, v_ref, uu, carry_ref, acc_ref, 0, row0, False)

        y = jnp.where(lane < SB_HEAD_DIM, acc_ref[0], acc_ref[1])
        y_ref[pl.ds(row0, SBA_TQ), :] = y.astype(BF16)
        return carry

    lax.fori_loop(0, seq // SBA_TQ, qblock, 0)


def _sba(sb_qkv, batch, seq):
    npair = SB_WIDTH // LANES
    blk = lambda off: pl.BlockSpec((None, seq, LANES), lambda b, p: (b, 0, off + p))
    return pl.pallas_call(
        functools.partial(_sba_kernel, seq=seq),
        grid=(batch, npair),
        in_specs=[blk(0), blk(npair), blk(2 * npair)],
        out_specs=pl.BlockSpec((None, seq, LANES), lambda b, p: (b, 0, p)),
        out_shape=jax.ShapeDtypeStruct((batch, seq, SB_WIDTH), BF16),
        scratch_shapes=[pltpu.VMEM((2, SBA_TQ, SBA_TK), F32), pltpu.VMEM((2, SBA_TQ, LANES), F32)],
        compiler_params=_params(),
        name="sba",
    )(sb_qkv, sb_qkv, sb_qkv)


def _conv_silu(x, w, b, row):
    y = x * w[CONV_WIDTH - 1:CONV_WIDTH, :] + b
    for k in range(1, CONV_WIDTH):
        shifted = jnp.where(row >= k, pltpu.roll(x, k, axis=0), 0.0)
        y = y + shifted * w[CONV_WIDTH - 1 - k:CONV_WIDTH - k, :]
    return y * jax.nn.sigmoid(y)


def _log_sigmoid(x):
    return -(jnp.maximum(-x, 0.0) + jnp.log1p(jnp.exp(-jnp.abs(x))))


def _mlstm_kernel(qk_ref, v_ref, o_ref, if_ref, cw_ref, cb_ref, ifb_ref, nw_ref, y_ref,
                  qs_ref, ks_ref, c_ref, n_ref, m_ref, *, seq):
    L = ML_CHUNK
    row = lax.broadcasted_iota(I32, (seq, LANES), 0)

    def conv_head(hd, carry):
        cq = pl.multiple_of(hd * LANES, LANES)
        ck = pl.multiple_of(ML_WIDTH + hd * LANES, LANES)
        q = _conv_silu(qk_ref[:, pl.ds(cq, LANES)], cw_ref[:, pl.ds(cq, LANES)], cb_ref[:, pl.ds(cq, LANES)], row)
        k = _conv_silu(qk_ref[:, pl.ds(ck, LANES)], cw_ref[:, pl.ds(ck, LANES)], cb_ref[:, pl.ds(ck, LANES)], row)
        qs_ref[hd] = q.astype(BF16)
        ks_ref[hd] = (k * (ML_HEAD_DIM ** -0.5)).astype(BF16)
        return carry

    lax.fori_loop(0, ML_HEADS, conv_head, 0)

    src = lax.broadcasted_iota(I32, (LANES, LANES), 0)
    rr = lax.broadcasted_iota(I32, (L, L), 0)
    cc = lax.broadcasted_iota(I32, (L, L), 1)
    tril = cc <= rr
    tril_b = jnp.where(tril, 1.0, 0.0).astype(BF16)
    eye = jnp.where(cc == rr, 1.0, 0.0)
    ones_b = jnp.ones((L, L), BF16)

    c_ref[...] = jnp.zeros_like(c_ref)
    n_ref[...] = jnp.zeros_like(n_ref)
    m_ref[...] = jnp.zeros_like(m_ref)

    def chunk(ci, carry):
        r0 = pl.multiple_of(ci * L, L)
        gates = if_ref[pl.ds(r0, L), :] + ifb_ref[...]
        for hd in range(ML_HEADS):
            lanes = slice(hd * LANES, (hd + 1) * LANES)
            ic = _dot_exact_lhs(gates, jnp.where(src == hd, 1.0, 0.0).astype(BF16))
            lf = _log_sigmoid(_dot_exact_lhs(gates, jnp.where(src == hd + ML_HEADS, 1.0, 0.0).astype(BF16)))
            qc = qs_ref[hd, pl.ds(r0, L), :]
            kc = ks_ref[hd, pl.ds(r0, L), :]
            vc = v_ref[pl.ds(r0, L), lanes]
            m = m_ref[hd]
            n = n_ref[hd]
            cmat = c_ref[hd]
            b = _dot_exact_rhs(tril_b, lf)
            drow = _dot_exact_rhs(ones_b, eye * (ic - b))
            dmat = jnp.where(tril, b + drow, -jnp.inf)
            inter = b + m
            m_t = jnp.maximum(inter, jnp.max(dmat, axis=-1, keepdims=True))
            w_intra = jnp.exp(dmat - m_t)
            w_inter = jnp.exp(inter - m_t)
            s = lax.dot_general(qc, kc, (((1,), (1,)), ((), ())), preferred_element_type=F32) * w_intra
            num = jnp.dot(s.astype(BF16), vc, preferred_element_type=F32)
            num = num + w_inter * jnp.dot(qc, cmat.astype(BF16), preferred_element_type=F32)
            qn = jnp.sum(qc.astype(F32) * n, axis=-1, keepdims=True)
            den = jnp.sum(s, axis=-1, keepdims=True) + w_inter * qn
            h = num / jnp.maximum(jnp.abs(den), jnp.exp(-m_t))
            hn = h * lax.rsqrt(jnp.mean(h * h, axis=-1, keepdims=True) + RMS_EPS) * nw_ref[:, lanes]
            y_ref[pl.ds(r0, L), lanes] = (jax.nn.sigmoid(o_ref[pl.ds(r0, L), lanes]) * hn).astype(BF16)
            b_last = b[L - 1:L, :]
            g = b_last - b + ic
            m_new = jnp.maximum(b_last + m, jnp.max(g, axis=0, keepdims=True))
            decay = jnp.exp(b_last + m - m_new)
            kw = kc.astype(F32) * jnp.exp(g - m_new)
            c_ref[hd] = decay * cmat + lax.dot_general(
                kw.astype(BF16), vc, (((0,), (0,)), ((), ())), preferred_element_type=F32)
            n_ref[hd] = decay * n + jnp.sum(kw, axis=0, keepdims=True)
            m_ref[hd] = m_new
        return carry

    lax.fori_loop(0, seq // L, chunk, 0)


def _mlstm(ml_qk, ml_v, ml_o, ml_if, conv_w, conv_b, ifb, nw, batch, seq):
    rows = lambda n: pl.BlockSpec((None, seq, n), lambda b: (b, 0, 0))
    full = lambda a: pl.BlockSpec(a.shape, lambda b: (0, 0))
    return pl.pallas_call(
        functools.partial(_mlstm_kernel, seq=seq),
        grid=(batch,),
        in_specs=[rows(2 * ML_WIDTH), rows(ML_WIDTH), rows(ML_WIDTH), rows(LANES),
                  full(conv_w), full(conv_b), full(ifb), full(nw)],
        out_specs=rows(ML_WIDTH),
        out_shape=jax.ShapeDtypeStruct((batch, seq, ML_WIDTH), BF16),
        scratch_shapes=[pltpu.VMEM((ML_HEADS, seq, LANES), BF16), pltpu.VMEM((ML_HEADS, seq, LANES), BF16),
                        pltpu.VMEM((ML_HEADS, LANES, LANES), F32),
                        pltpu.VMEM((ML_HEADS, 1, LANES), F32), pltpu.VMEM((ML_HEADS, 1, LANES), F32)],
        compiler_params=_params(),
        name="mlstm",
    )(ml_qk, ml_v, ml_o, ml_if, conv_w, conv_b, ifb, nw)


def _to_slabs(a):
    t = a.shape[0]
    parts = [a[:, LANES * r:LANES * (r + 1)] for r in range(SUBLANES)]
    return jnp.stack(parts, axis=1).reshape(t * SUBLANES, LANES)


def _from_slabs(a):
    t = a.shape[0] // SUBLANES
    y = a.reshape(t, SUBLANES, LANES)
    return jnp.concatenate([y[:, r, :] for r in range(SUBLANES)], axis=1)


def _merge_kernel(ysb, yml, g_ref, x_ref, wsb, wml, wout, nw_ref, wq, h_ref, hn_ref, q_ref):
    a = jnp.dot(ysb[...], wsb[...], preferred_element_type=F32)
    b = jnp.dot(yml[...], wml[...], preferred_element_type=F32)
    g = g_ref[...]
    merged = jax.nn.sigmoid(g[:, :D_MODEL]) * a + jax.nn.sigmoid(g[:, D_MODEL:]) * b
    h = x_ref[...] + jnp.dot(merged.astype(BF16), wout[...], preferred_element_type=F32)
    h_ref[...] = _to_slabs(h)
    hn = h * lax.rsqrt(jnp.mean(h * h, axis=-1, keepdims=True) + RMS_EPS) * nw_ref[...]
    hn_ref[...] = _to_slabs(hn)
    q_ref[...] = jnp.dot(hn.astype(BF16), wq[...], preferred_element_type=F32).astype(BF16)


def _merge(ysb, yml, gates, xt, wsb, wml, wout, nw, wq):
    t = xt.shape[0]
    tm = min(TM_PROJ, t)
    full = lambda a: pl.BlockSpec(a.shape, lambda i: (0, 0))
    tok = lambda n: pl.BlockSpec((tm, n), lambda i: (i, 0))
    slab = pl.BlockSpec((tm * SUBLANES, LANES), lambda i: (i, 0))
    nq = wq.shape[1]
    return pl.pallas_call(
        _merge_kernel,
        grid=(t // tm,),
        in_specs=[tok(SB_WIDTH), tok(ML_WIDTH), tok(2 * D_MODEL), tok(D_MODEL),
                  full(wsb), full(wml), full(wout), full(nw), full(wq)],
        out_specs=[slab, slab, tok(nq)],
        out_shape=[jax.ShapeDtypeStruct((t * SUBLANES, LANES), F32),
                   jax.ShapeDtypeStruct((t * SUBLANES, LANES), F32),
                   jax.ShapeDtypeStruct((t, nq), BF16)],
        compiler_params=_params(),
        name="merge",
    )(ysb, yml, gates, xt, wsb, wml, wout, nw, wq)


_BIG = 3.0e38


def _extract_top(vals, payload, val_ref, pay_ref):
    for it in range(PEER_TOPK):
        m = jnp.max(vals, axis=0, keepdims=True)
        p = jnp.min(jnp.where(vals == m, payload, _BIG), axis=0, keepdims=True)
        val_ref[it:it + 1, :] = m
        pay_ref[it:it + 1, :] = p
        vals = jnp.where(payload == p, -jnp.inf, vals)


def _topk_kernel(q_ref, k1_ref, k2_ref, off_ref, gate_ref, v1, i1, v2, i2, tv, te, gall, *, tt):
    dn = (((1,), (1,)), ((), ()))
    key_iota = lax.broadcasted_iota(I32, (PEER_N_KEYS, tt), 0).astype(F32)
    a8 = lax.broadcasted_iota(I32, (SUBLANES, tt), 0)
    for h in range(PEER_HEADS):
        q1 = q_ref[:, 2 * h * PEER_HALF:(2 * h + 1) * PEER_HALF]
        q2 = q_ref[:, (2 * h + 1) * PEER_HALF:(2 * h + 2) * PEER_HALF]
        s1 = lax.dot_general(k1_ref[h], q1, dn, preferred_element_type=F32)
        s2 = lax.dot_general(k2_ref[h], q2, dn, preferred_element_type=F32)
        _extract_top(s1, key_iota, v1, i1)
        _extract_top(s2, key_iota, v2, i2)
        va, vb = v1[...], v2[...]
        ea, eb = i1[...] * float(PEER_N_KEYS), i2[...]
        lo8 = slice(0, SUBLANES)
        hi8 = slice(SUBLANES, 2 * SUBLANES)
        cv = [va[0:1] + vb, va[1:2] + vb[lo8], va[hi8] + vb[0:1]]
        ce = [ea[0:1] + eb, ea[1:2] + eb[lo8], ea[hi8] + eb[0:1]]
        for bcol, amax in ((0, 7), (1, 7), (2, 4), (3, 3), (4, 2)):
            ok = (a8 >= 2) & (a8 <= amax)
            cv.append(jnp.where(ok, va[lo8] + vb[bcol:bcol + 1], -jnp.inf))
            ce.append(ea[lo8] + eb[bcol:bcol + 1])
        _extract_top(jnp.concatenate(cv, axis=0), jnp.concatenate(ce, axis=0), tv, te)
        sc = tv[...]
        ex = jnp.exp(sc - jnp.max(sc, axis=0, keepdims=True))
        rows = slice(h * PEER_TOPK, (h + 1) * PEER_TOPK)
        gall[rows, :] = ex / jnp.sum(ex, axis=0, keepdims=True)
        off_ref[rows, :] = te[...].astype(I32) * WORD_ROWS
    gate_ref[...] = jnp.transpose(gall[...])


def _topk(q, k1, k2):
    t = q.shape[0]
    tt = min(TT_TOPK, t)
    kspec = pl.BlockSpec(k1.shape, lambda i: (0, 0, 0))
    small = lambda: pltpu.VMEM((PEER_TOPK, tt), F32)
    tok = pl.BlockSpec((tt, N_SEL), lambda i: (i, 0))
    return pl.pallas_call(
        functools.partial(_topk_kernel, tt=tt),
        grid=(t // tt,),
        in_specs=[pl.BlockSpec((tt, q.shape[1]), lambda i: (i, 0)), kspec, kspec],
        out_specs=[pl.BlockSpec((N_SEL, tt), lambda i: (0, i)), tok],
        out_shape=[jax.ShapeDtypeStruct((N_SEL, t), I32), jax.ShapeDtypeStruct((t, N_SEL), F32)],
        scratch_shapes=[small(), small(), small(), small(), small(), small(),
                        pltpu.VMEM((N_SEL, tt), F32)],
        compiler_params=_params(),
        name="topk",
    )(q, k1, k2)


def _off_copy(off_hbm, buf, sems, half_idx, s):
    return pltpu.make_async_copy(off_hbm.at[:, pl.ds(half_idx * PEER_HALF_TOK, PEER_HALF_TOK)], buf, sems.at[s])


def _run_halves(off_hbm, tab_hbm, off_a, off_b, tab_vmem, sems, run_half):
    i = pl.program_id(0)
    n = pl.num_programs(0)

    @pl.when(i == 0)
    def _():
        cp = pltpu.make_async_copy(tab_hbm, tab_vmem, sems.at[2])
        cp.start()
        _off_copy(off_hbm, off_a, sems, 0, 0).start()
        _off_copy(off_hbm, off_b, sems, 1, 1).start()
        cp.wait()

    for s, buf in enumerate((off_a, off_b)):
        _off_copy(off_hbm, buf, sems, 2 * i + s, s).wait()
        run_half(buf, s * PEER_HALF_TOK)

        @pl.when(i + 1 < n)
        def _():
            _off_copy(off_hbm, buf, sems, 2 * (i + 1) + s, s).start()


def _gather_token(off_smem, tab_vmem, col):
    rows = []
    for j in range(N_SEL):
        off = pl.multiple_of(off_smem.at[j][col], WORD_ROWS)
        rows.append(tab_vmem[pl.ds(off, WORD_ROWS), :])
    return pltpu.bitcast(jnp.concatenate(rows, axis=0), BF16)


def _slab_mask():
    s = lax.broadcasted_iota(I32, (SUBLANES, GATHER_ROWS), 0)
    n = lax.broadcasted_iota(I32, (SUBLANES, GATHER_ROWS), 1)
    return (n & (SUBLANES - 1)) == s


def _fold_matrix(shape, row_axis):
    n = lax.broadcasted_iota(I32, shape, row_axis)
    j = lax.broadcasted_iota(I32, shape, 1 - row_axis)
    return jnp.where(n // EXPERT_ROWS == j, 1.0, 0.0).astype(BF16)


def _merge_halves(a, b, m, k):
    r = pltpu.roll(b, k, axis=0)
    c = jnp.where(m, a, r)
    w = jnp.where(m, r, a)
    return c + pltpu.roll(w, SUBLANES - k, axis=0)


def _sublane_sums(xs, sub):
    m4, m2, m1 = sub < 4, (sub & 2) == 0, (sub & 1) == 0
    s = [_merge_halves(xs[a], xs[a + 4], m4, 4) for a in (0, 2, 1, 3)]
    return _merge_halves(_merge_halves(s[0], s[1], m2, 2), _merge_halves(s[2], s[3], m2, 2), m1, 1)


def _peer_u_kernel(off_hbm, x_ref, gate_ref, tab_hbm, w_ref, off_a, off_b, tab_vmem, cs_ref, sems):
    subw = lax.broadcasted_iota(I32, (SUBLANES, GATHER_ROWS), 0)
    mask = _slab_mask()
    dn = (((1,), (1,)), ((), ()))

    def run_half(off_smem, row_base):
        def group(g, carry):
            for oct_ in range(PEER_GROUP // SUBLANES):
                parts = []
                for tt in range(SUBLANES):
                    col = g * PEER_GROUP + oct_ * SUBLANES + tt
                    m = _gather_token(off_smem, tab_vmem, col)
                    r8 = pl.multiple_of((row_base + col) * SUBLANES, SUBLANES)
                    xh, xl = _split2(x_ref[pl.ds(r8, SUBLANES), :])
                    p = lax.dot_general(jnp.concatenate([xh, xl], axis=0), m, dn, preferred_element_type=F32)
                    parts.append(jnp.where(mask, p[:SUBLANES] + p[SUBLANES:], 0.0))
                r0 = pl.multiple_of(row_base + g * PEER_GROUP + oct_ * SUBLANES, SUBLANES)
                cs_ref[pl.ds(r0, SUBLANES), :] = _sublane_sums(parts, subw)
            return carry

        lax.fori_loop(0, PEER_HALF_TOK // PEER_GROUP, group, 0)

    _run_halves(off_hbm, tab_hbm, off_a, off_b, tab_vmem, sems, run_half)
    acts = _dot_exact_lhs(cs_ref[...], _fold_matrix((GATHER_ROWS, N_SEL), 0))
    w_ref[...] = gate_ref[...] * (0.5 * acts * (1.0 + lax.erf(acts * (2.0 ** -0.5))))


def _peer_scratch(tb, *extra):
    return [pltpu.SMEM((N_SEL, PEER_HALF_TOK), I32), pltpu.SMEM((N_SEL, PEER_HALF_TOK), I32),
            pltpu.VMEM((TAB_ROWS, LANES), I32), pltpu.VMEM((tb, GATHER_ROWS), F32), *extra,
            pltpu.SemaphoreType.DMA((3,))]


def _peer_u(off, x8, gate, tab):
    t = gate.shape[0]
    tb = TB_PEER
    tok = pl.BlockSpec((tb, N_SEL), lambda i: (i, 0))
    return pl.pallas_call(
        _peer_u_kernel,
        grid=(t // tb,),
        in_specs=[pl.BlockSpec(memory_space=pl.ANY),
                  pl.BlockSpec((tb * SUBLANES, LANES), lambda i: (i, 0)),
                  tok, pl.BlockSpec(memory_space=pl.ANY)],
        out_specs=tok,
        out_shape=jax.ShapeDtypeStruct((t, N_SEL), F32),
        scratch_shapes=_peer_scratch(tb),
        compiler_params=_params(),
        name="peer_u",
    )(off, x8, gate, tab)


def _peer_v_kernel(off_hbm, w_ref, h_ref, nw_ref, tab_hbm, out_ref, off_a, off_b, tab_vmem, wrep_ref, y_ref, sems):
    mask = _slab_mask()
    wrep_ref[...] = _dot_exact_lhs(w_ref[...], _fold_matrix((N_SEL, GATHER_ROWS), 1))

    def run_half(off_smem, row_base):
        def group(g, carry):
            for oct_ in range(PEER_GROUP // SUBLANES):
                r0 = pl.multiple_of(row_base + g * PEER_GROUP + oct_ * SUBLANES, SUBLANES)
                wrep = wrep_ref[pl.ds(r0, SUBLANES), :]
                for tt in range(SUBLANES):
                    col = g * PEER_GROUP + oct_ * SUBLANES + tt
                    m = _gather_token(off_smem, tab_vmem, col)
                    ah, al = _split2(jnp.where(mask, wrep[tt:tt + 1, :], 0.0))
                    res = jnp.dot(jnp.concatenate([ah, al], axis=0), m, preferred_element_type=F32)
                    r8 = pl.multiple_of((row_base + col) * SUBLANES, SUBLANES)
                    y = h_ref[pl.ds(r8, SUBLANES), :] + res[:SUBLANES] + res[SUBLANES:]
                    ss = jnp.sum(jnp.sum(y * y, axis=1, keepdims=True), axis=0, keepdims=True)
                    y_ref[pl.ds(r8, SUBLANES), :] = (
                        y * lax.rsqrt(ss * (1.0 / D_MODEL) + RMS_EPS) * nw_ref[...])
            return carry

        lax.fori_loop(0, PEER_HALF_TOK // PEER_GROUP, group, 0)

    _run_halves(off_hbm, tab_hbm, off_a, off_b, tab_vmem, sems, run_half)
    out_ref[...] = _from_slabs(y_ref[...])


def _peer_v(off, w, h8, nw8, tab):
    t = w.shape[0]
    tb = TB_PEER
    tok = pl.BlockSpec((tb, N_SEL), lambda i: (i, 0))
    return pl.pallas_call(
        _peer_v_kernel,
        grid=(t // tb,),
        in_specs=[pl.BlockSpec(memory_space=pl.ANY), tok,
                  pl.BlockSpec((tb * SUBLANES, LANES), lambda i: (i, 0)),
                  pl.BlockSpec((SUBLANES, LANES), lambda i: (0, 0)),
                  pl.BlockSpec(memory_space=pl.ANY)],
        out_specs=pl.BlockSpec((tb, D_MODEL), lambda i: (i, 0)),
        out_shape=jax.ShapeDtypeStruct((t, D_MODEL), F32),
        scratch_shapes=_peer_scratch(tb, pltpu.VMEM((tb * SUBLANES, LANES), F32)),
        compiler_params=_params(),
        name="peer_v",
    )(off, w, h8, nw8, tab)


def _table_kernel(w_ref, o_ref):
    o_ref[...] = pltpu.bitcast(_to_slabs(w_ref[...]).astype(BF16), I32)


def _expert_table(w):
    n = w.shape[0]
    eb = TABLE_BLOCK
    return pl.pallas_call(
        _table_kernel,
        grid=(n // eb,),
        in_specs=[pl.BlockSpec((eb, D_MODEL), lambda i: (i, 0))],
        out_specs=pl.BlockSpec((eb * WORD_ROWS, LANES), lambda i: (i, 0)),
        out_shape=jax.ShapeDtypeStruct((n * WORD_ROWS, LANES), I32),
        compiler_params=_params(),
        name="expert_table",
    )(w)


def kernel(x, norm_mix_w, w_in, conv_w, conv_b, b_igate, b_fgate, mlstm_norm_w, w_branch_sb, w_branch_ml, w_out, norm_ffn_w, peer_w_query, peer_keys1, peer_keys2, peer_u, peer_v, norm_final_w):
    batch, seq, d = x.shape
    t = batch * seq
    assert d == D_MODEL and w_in.shape[0] == 1
    assert seq % SBA_TQ == 0 and seq % ML_CHUNK == 0 and t % TM_PROJ == 0 and t % TT_TOPK == 0 and t % TB_PEER == 0
    xt = x.reshape(t, d)

    o_sb = 3 * SB_WIDTH
    o_qk = o_sb + 2 * ML_WIDTH
    o_v = o_qk + ML_WIDTH
    o_o = o_v + ML_WIDTH
    o_if = o_o + 2 * ML_HEADS
    w = w_in[0]
    wb = lambda a: a.astype(BF16)
    w_if = jnp.pad(w[:, o_o:o_if], ((0, 0), (0, LANES - 2 * ML_HEADS)))
    sb_qkv, ml_qk, ml_v, ml_o, gates, ml_if = _in_proj(
        xt, norm_mix_w[0].reshape(1, d), wb(w[:, :o_sb]), wb(w[:, o_sb:o_qk]), wb(w[:, o_qk:o_v]),
        wb(w[:, o_v:o_o]), wb(w[:, o_if:]), wb(w_if))

    y_sb = _sba(sb_qkv.reshape(batch, seq, 3 * SB_WIDTH), batch, seq)

    ifb = jnp.pad(jnp.concatenate([b_igate[0], b_fgate[0]]), (0, LANES - 2 * ML_HEADS)).reshape(1, LANES)
    y_ml = _mlstm(ml_qk.reshape(batch, seq, 2 * ML_WIDTH), ml_v.reshape(batch, seq, ML_WIDTH),
                  ml_o.reshape(batch, seq, ML_WIDTH), ml_if.reshape(batch, seq, LANES),
                  conv_w[0], conv_b[0].reshape(1, 2 * ML_WIDTH), ifb.astype(F32),
                  mlstm_norm_w[0].reshape(1, ML_WIDTH), batch, seq)

    h, hn, q = _merge(y_sb.reshape(t, SB_WIDTH), y_ml.reshape(t, ML_WIDTH), gates, xt,
                      wb(w_branch_sb[0]), wb(w_branch_ml[0]), wb(w_out[0]),
                      norm_ffn_w[0].reshape(1, d), wb(peer_w_query[0]))

    off, gate = _topk(q, wb(peer_keys1[0]), wb(peer_keys2[0]))

    wsel = _peer_u(off, hn, gate, _expert_table(peer_u[0]))

    nw8 = norm_final_w.reshape(SUBLANES, LANES)
    out = _peer_v(off, wsel, h, nw8, _expert_table(peer_v[0]))
    return out.reshape(batch, seq, d)
```

```python
import functools

import jax
import jax.numpy as jnp
from jax import lax
from jax.experimental import pallas as pl
from jax.experimental.pallas import tpu as pltpu

F32 = jnp.float32
BF16 = jnp.bfloat16
I32 = jnp.int32

D_MODEL = 1024
SB_HEADS = 8
SB_HEAD_DIM = 64
SB_WIDTH = SB_HEADS * SB_HEAD_DIM
ML_HEADS = 4
ML_HEAD_DIM = 128
ML_WIDTH = ML_HEADS * ML_HEAD_DIM
CONV_WIDTH = 4
PEER_HEADS = 8
PEER_N_KEYS = 128
PEER_N_EXPERTS = PEER_N_KEYS * PEER_N_KEYS
PEER_HALF = 128
PEER_TOPK = 16
N_SEL = PEER_HEADS * PEER_TOPK
RMS_EPS = 1e-6

LANES = 128
SUBLANES = 8
VMEM_LIMIT = 48 * 1024 * 1024

TM_PROJ = 256
SBA_TQ = 256
SBA_TK = 128
SBA_TK2 = 2 * SBA_TK
ML_CHUNK = 128
TT_TOPK = 256
PEER_HALF_TOK = 128
TB_PEER = 2 * PEER_HALF_TOK
PEER_GROUP = 32
EXPERT_ROWS = SUBLANES
WORD_ROWS = EXPERT_ROWS // 2
GATHER_ROWS = N_SEL * EXPERT_ROWS
TAB_ROWS = PEER_N_EXPERTS * WORD_ROWS
TABLE_BLOCK = 256


def _params():
    return pltpu.CompilerParams(dimension_semantics=None, vmem_limit_bytes=VMEM_LIMIT)


def _split3(a):
    hi = a.astype(BF16)
    r = a - hi.astype(F32)
    mid = r.astype(BF16)
    lo = (r - mid.astype(F32)).astype(BF16)
    return hi, mid, lo


def _split2(a):
    hi = a.astype(BF16)
    return hi, (a - hi.astype(F32)).astype(BF16)


def _dot_exact_rhs(a_bf16, b_f32):
    out = None
    for piece in _split3(b_f32):
        d = jnp.dot(a_bf16, piece, preferred_element_type=F32)
        out = d if out is None else out + d
    return out


def _dot_exact_lhs(a_f32, b_bf16):
    out = None
    for piece in _split3(a_f32):
        d = jnp.dot(piece, b_bf16, preferred_element_type=F32)
        out = d if out is None else out + d
    return out


def _inproj_kernel(x_ref, nw_ref, wsb, wqk, wv, wo, wg, wif, osb, oqk, ov, oo, og, oif):
    x = x_ref[...]
    ms = jnp.mean(x * x, axis=-1, keepdims=True)
    xb = (x * lax.rsqrt(ms + RMS_EPS) * nw_ref[...]).astype(BF16)
    osb[...] = jnp.dot(xb, wsb[...], preferred_element_type=F32).astype(BF16)
    oqk[...] = jnp.dot(xb, wqk[...], preferred_element_type=F32)
    ov[...] = jnp.dot(xb, wv[...], preferred_element_type=F32).astype(BF16)
    oo[...] = jnp.dot(xb, wo[...], preferred_element_type=F32)
    og[...] = jnp.dot(xb, wg[...], preferred_element_type=F32)
    oif[...] = jnp.dot(xb, wif[...], preferred_element_type=F32)


def _in_proj(xt, nw, wsb, wqk, wv, wo, wg, wif):
    t = xt.shape[0]
    tm = min(TM_PROJ, t)
    full = lambda a: pl.BlockSpec(a.shape, lambda i: (0, 0))
    tok = lambda n: pl.BlockSpec((tm, n), lambda i: (i, 0))
    ws = (wsb, wqk, wv, wo, wg, wif)
    outs = [(wsb.shape[1], BF16), (wqk.shape[1], F32), (wv.shape[1], BF16),
            (wo.shape[1], F32), (wg.shape[1], F32), (wif.shape[1], F32)]
    return pl.pallas_call(
        _inproj_kernel,
        grid=(t // tm,),
        in_specs=[tok(D_MODEL), full(nw)] + [full(w) for w in ws],
        out_specs=[tok(n) for n, _ in outs],
        out_shape=[jax.ShapeDtypeStruct((t, n), dt) for n, dt in outs],
        compiler_params=_params(),
        name="in_proj",
    )(xt, nw, *ws)


def _sba_step(qms, k_ref, v_ref, uu, carry_ref, acc_ref, kb2, row0, masked):
    c0 = pl.multiple_of(kb2 * SBA_TK2, SBA_TK2)
    kblk = k_ref[pl.ds(c0, SBA_TK2), :]
    vblk = v_ref[pl.ds(c0, SBA_TK2), :]
    if masked:
        rows = row0 + lax.broadcasted_iota(I32, (SBA_TQ, SBA_TK2), 0)
        cols = c0 + lax.broadcasted_iota(I32, (SBA_TQ, SBA_TK2), 1)
        causal = cols < rows
    for hh in range(2):
        z = lax.dot_general(qms[hh], kblk, (((1,), (1,)), ((), ())), preferred_element_type=F32)
        sp = jnp.maximum(z, 0.0) + jnp.log(1.0 + jnp.exp(-jnp.abs(z)))
        if masked:
            sp = jnp.where(causal, sp, 0.0)
        hi, lo = _split2(sp)
        su_lo = jnp.dot(jnp.concatenate([hi[:, :SBA_TK], lo[:, :SBA_TK]], axis=1), uu,
                        preferred_element_type=F32)
        su_hi = jnp.dot(jnp.concatenate([hi[:, SBA_TK:], lo[:, SBA_TK:]], axis=1), uu,
                        preferred_element_type=F32)
        carry = carry_ref[hh]
        later = jnp.concatenate([su_lo[:, :SBA_TK] + su_hi[:, SBA_TK:], su_hi[:, :SBA_TK]], axis=1)
        w = jnp.exp(z - sp - later - jnp.concatenate([carry, carry], axis=1))
        if masked:
            w = jnp.where(causal, w, 0.0)
        acc_ref[hh] += jnp.dot(w.astype(BF16), vblk, preferred_element_type=F32)
        carry_ref[hh] = carry + su_lo[:, SBA_TK:] + su_hi[:, SBA_TK:]


def _sba_kernel(q_ref, k_ref, v_ref, y_ref, carry_ref, acc_ref, *, seq):
    r = lax.broadcasted_iota(I32, (SBA_TK2, SBA_TK2), 0) % SBA_TK
    c = lax.broadcasted_iota(I32, (SBA_TK2, SBA_TK2), 1)
    uu = jnp.where((c >= SBA_TK) | (r > c), 1.0, 0.0).astype(BF16)
    lane = lax.broadcasted_iota(I32, (SBA_TQ, LANES), 1)
    assert SBA_TQ == SBA_TK2

    def qblock(qi, carry):
        row0 = pl.multiple_of(qi * SBA_TQ, SBA_TQ)
        qf = q_ref[pl.ds(row0, SBA_TQ), :] * jnp.asarray(SB_HEAD_DIM ** -0.5, BF16)
        zero = jnp.zeros_like(qf)
        qms = [jnp.where(lane < SB_HEAD_DIM, qf, zero), jnp.where(lane >= SB_HEAD_DIM, qf, zero)]
        carry_ref[...] = jnp.zeros_like(carry_ref)
        acc_ref[...] = jnp.zeros_like(acc_ref)
        _sba_step(qms, k_ref, v_ref, uu, carry_ref, acc_ref, qi, row0, True)

        def below(kk, c2):
            for d in range(2):
                _sba_step(qms, k_ref, v_ref, uu, carry_ref, acc_ref, qi - 1 - 2 * kk - d, row0, False)
            return c2

        lax.fori_loop(0, qi // 2, below, 0)

        @pl.when(qi % 2 == 1)
        def _():
            _sba_step(qms, k_ref, v_ref, uu, carry_ref, acc_ref, 0, row0, False)

        y = jnp.where(lane < SB_HEAD_DIM, acc_ref[0], acc_ref[1])
        y_ref[pl.ds(row0, SBA_TQ), :] = y.astype(BF16)
        return carry

    lax.fori_loop(0, seq // SBA_TQ, qblock, 0)


def _sba(sb_qkv, batch, seq):
    npair = SB_WIDTH // LANES
    blk = lambda off: pl.BlockSpec((None, seq, LANES), lambda b, p: (b, 0, off + p))
    return pl.pallas_call(
        functools.partial(_sba_kernel, seq=seq),
        grid=(batch, npair),
        in_specs=[blk(0), blk(npair), blk(2 * npair)],
        out_specs=pl.BlockSpec((None, seq, LANES), lambda b, p: (b, 0, p)),
        out_shape=jax.ShapeDtypeStruct((batch, seq, SB_WIDTH), BF16),
        scratch_shapes=[pltpu.VMEM((2, SBA_TQ, SBA_TK), F32), pltpu.VMEM((2, SBA_TQ, LANES), F32)],
        compiler_params=_params(),
        name="sba",
    )(sb_qkv, sb_qkv, sb_qkv)


def _conv_silu(x, w, b, row):
    y = x * w[CONV_WIDTH - 1:CONV_WIDTH, :] + b
    for k in range(1, CONV_WIDTH):
        shifted = jnp.where(row >= k, pltpu.roll(x, k, axis=0), 0.0)
        y = y + shifted * w[CONV_WIDTH - 1 - k:CONV_WIDTH - k, :]
    return y * jax.nn.sigmoid(y)


def _log_sigmoid(x):
    return -(jnp.maximum(-x, 0.0) + jnp.log1p(jnp.exp(-jnp.abs(x))))


def _mlstm_kernel(qk_ref, v_ref, o_ref, if_ref, cw_ref, cb_ref, ifb_ref, nw_ref, y_ref,
                  qs_ref, ks_ref, c_ref, n_ref, m_ref, *, seq):
    L = ML_CHUNK
    row = lax.broadcasted_iota(I32, (seq, LANES), 0)

    def conv_head(hd, carry):
        cq = pl.multiple_of(hd * LANES, LANES)
        ck = pl.multiple_of(ML_WIDTH + hd * LANES, LANES)
        q = _conv_silu(qk_ref[:, pl.ds(cq, LANES)], cw_ref[:, pl.ds(cq, LANES)], cb_ref[:, pl.ds(cq, LANES)], row)
        k = _conv_silu(qk_ref[:, pl.ds(ck, LANES)], cw_ref[:, pl.ds(ck, LANES)], cb_ref[:, pl.ds(ck, LANES)], row)
        qs_ref[hd] = q.astype(BF16)
        ks_ref[hd] = (k * (ML_HEAD_DIM ** -0.5)).astype(BF16)
        return carry

    lax.fori_loop(0, ML_HEADS, conv_head, 0)

    src = lax.broadcasted_iota(I32, (LANES, LANES), 0)
    rr = lax.broadcasted_iota(I32, (L, L), 0)
    cc = lax.broadcasted_iota(I32, (L, L), 1)
    tril = cc <= rr
    tril_b = jnp.where(tril, 1.0, 0.0).astype(BF16)
    eye = jnp.where(cc == rr, 1.0, 0.0)
    ones_b = jnp.ones((L, L), BF16)

    c_ref[...] = jnp.zeros_like(c_ref)
    n_ref[...] = jnp.zeros_like(n_ref)
    m_ref[...] = jnp.zeros_like(m_ref)

    def chunk(ci, carry):
        r0 = pl.multiple_of(ci * L, L)
        gates = if_ref[pl.ds(r0, L), :] + ifb_ref[...]
        log_f = _log_sigmoid(gates)
        for hd in range(ML_HEADS):
            lanes = slice(hd * LANES, (hd + 1) * LANES)
            ic = _dot_exact_lhs(gates, jnp.where(src == hd, 1.0, 0.0).astype(BF16))
            lf = _dot_exact_lhs(log_f, jnp.where(src == hd + ML_HEADS, 1.0, 0.0).astype(BF16))
            qc = qs_ref[hd, pl.ds(r0, L), :]
            kc = ks_ref[hd, pl.ds(r0, L), :]
            vc = v_ref[pl.ds(r0, L), lanes]
            m = m_ref[hd]
            n = n_ref[hd]
            cmat = c_ref[hd]
            b = _dot_exact_rhs(tril_b, lf)
            drow = _dot_exact_rhs(ones_b, eye * (ic - b))
            dmat = jnp.where(tril, b + drow, -jnp.inf)
            inter = b + m
            m_t = jnp.maximum(inter, jnp.max(dmat, axis=-1, keepdims=True))
            w_intra = jnp.exp(dmat - m_t)
            w_inter = jnp.exp(inter - m_t)
            s = lax.dot_general(qc, kc, (((1,), (1,)), ((), ())), preferred_element_type=F32) * w_intra
            num = jnp.dot(s.astype(BF16), vc, preferred_element_type=F32)
            num = num + w_inter * jnp.dot(qc, cmat.astype(BF16), preferred_element_type=F32)
            qn = jnp.sum(qc.astype(F32) * n, axis=-1, keepdims=True)
            den = jnp.sum(s, axis=-1, keepdims=True) + w_inter * qn
            h = num / jnp.maximum(jnp.abs(den), jnp.exp(-m_t))
            hn = h * lax.rsqrt(jnp.mean(h * h, axis=-1, keepdims=True) + RMS_EPS) * nw_ref[:, lanes]
            y_ref[pl.ds(r0, L), lanes] = (jax.nn.sigmoid(o_ref[pl.ds(r0, L), lanes]) * hn).astype(BF16)
            b_last = b[L - 1:L, :]
            g = b_last - b + ic
            m_new = jnp.maximum(b_last + m, jnp.max(g, axis=0, keepdims=True))
            decay = jnp.exp(b_last + m - m_new)
            kw = kc.astype(F32) * jnp.exp(g - m_new)
            c_ref[hd] = decay * cmat + lax.dot_general(
                kw.astype(BF16), vc, (((0,), (0,)), ((), ())), preferred_element_type=F32)
            n_ref[hd] = decay * n + jnp.sum(kw, axis=0, keepdims=True)
            m_ref[hd] = m_new
        return carry

    lax.fori_loop(0, seq // L, chunk, 0)


def _mlstm(ml_qk, ml_v, ml_o, ml_if, conv_w, conv_b, ifb, nw, batch, seq):
    rows = lambda n: pl.BlockSpec((None, seq, n), lambda b: (b, 0, 0))
    full = lambda a: pl.BlockSpec(a.shape, lambda b: (0, 0))
    return pl.pallas_call(
        functools.partial(_mlstm_kernel, seq=seq),
        grid=(batch,),
        in_specs=[rows(2 * ML_WIDTH), rows(ML_WIDTH), rows(ML_WIDTH), rows(LANES),
                  full(conv_w), full(conv_b), full(ifb), full(nw)],
        out_specs=rows(ML_WIDTH),
        out_shape=jax.ShapeDtypeStruct((batch, seq, ML_WIDTH), BF16),
        scratch_shapes=[pltpu.VMEM((ML_HEADS, seq, LANES), BF16), pltpu.VMEM((ML_HEADS, seq, LANES), BF16),
                        pltpu.VMEM((ML_HEADS, LANES, LANES), F32),
                        pltpu.VMEM((ML_HEADS, 1, LANES), F32), pltpu.VMEM((ML_HEADS, 1, LANES), F32)],
        compiler_params=_params(),
        name="mlstm",
    )(ml_qk, ml_v, ml_o, ml_if, conv_w, conv_b, ifb, nw)


def _to_slabs(a):
    t = a.shape[0]
    parts = [a[:, LANES * r:LANES * (r + 1)] for r in range(SUBLANES)]
    return jnp.stack(parts, axis=1).reshape(t * SUBLANES, LANES)


def _from_slabs(a):
    t = a.shape[0] // SUBLANES
    y = a.reshape(t, SUBLANES, LANES)
    return jnp.concatenate([y[:, r, :] for r in range(SUBLANES)], axis=1)


def _merge_kernel(ysb, yml, g_ref, x_ref, wsb, wml, wout, nw_ref, wq, h_ref, hn_ref, q_ref):
    a = jnp.dot(ysb[...], wsb[...], preferred_element_type=F32)
    b = jnp.dot(yml[...], wml[...], preferred_element_type=F32)
    g = g_ref[...]
    merged = jax.nn.sigmoid(g[:, :D_MODEL]) * a + jax.nn.sigmoid(g[:, D_MODEL:]) * b
    h = x_ref[...] + jnp.dot(merged.astype(BF16), wout[...], preferred_element_type=F32)
    h_ref[...] = _to_slabs(h)
    hn = h * lax.rsqrt(jnp.mean(h * h, axis=-1, keepdims=True) + RMS_EPS) * nw_ref[...]
    hn_ref[...] = _to_slabs(hn)
    q_ref[...] = jnp.dot(hn.astype(BF16), wq[...], preferred_element_type=F32).astype(BF16)


def _merge(ysb, yml, gates, xt, wsb, wml, wout, nw, wq):
    t = xt.shape[0]
    tm = min(TM_PROJ, t)
    full = lambda a: pl.BlockSpec(a.shape, lambda i: (0, 0))
    tok = lambda n: pl.BlockSpec((tm, n), lambda i: (i, 0))
    slab = pl.BlockSpec((tm * SUBLANES, LANES), lambda i: (i, 0))
    nq = wq.shape[1]
    return pl.pallas_call(
        _merge_kernel,
        grid=(t // tm,),
        in_specs=[tok(SB_WIDTH), tok(ML_WIDTH), tok(2 * D_MODEL), tok(D_MODEL),
                  full(wsb), full(wml), full(wout), full(nw), full(wq)],
        out_specs=[slab, slab, tok(nq)],
        out_shape=[jax.ShapeDtypeStruct((t * SUBLANES, LANES), F32),
                   jax.ShapeDtypeStruct((t * SUBLANES, LANES), F32),
                   jax.ShapeDtypeStruct((t, nq), BF16)],
        compiler_params=_params(),
        name="merge",
    )(ysb, yml, gates, xt, wsb, wml, wout, nw, wq)


_BIG = 3.0e38


def _extract_top(vals, payload, val_ref, pay_ref):
    for it in range(PEER_TOPK):
        m = jnp.max(vals, axis=0, keepdims=True)
        p = jnp.min(jnp.where(vals == m, payload, _BIG), axis=0, keepdims=True)
        val_ref[it:it + 1, :] = m
        pay_ref[it:it + 1, :] = p
        vals = jnp.where(payload == p, -jnp.inf, vals)


def _batcher_pairs(n):
    pairs = []
    p = 1
    while p < n:
        k = p
        while k >= 1:
            for j in range(k % p, n - k, 2 * k):
                for i in range(min(k, n - j - k)):
                    if (i + j) // (2 * p) == (i + j + k) // (2 * p):
                        pairs.append((i + j, i + j + k))
            k //= 2
        p *= 2
    return pairs


def _extract_top_keys(s, val_ref, pay_ref):
    nv = PEER_N_KEYS // SUBLANES
    sub = lax.broadcasted_iota(I32, (SUBLANES, s.shape[1]), 0).astype(F32)
    vals = [s[SUBLANES * k:SUBLANES * (k + 1), :] for k in range(nv)]
    idxs = [sub + float(SUBLANES * k) for k in range(nv)]
    for x, y in _batcher_pairs(nv):
        a, b, ia, ib = vals[x], vals[y], idxs[x], idxs[y]
        keep = a >= b
        vals[x], vals[y] = jnp.where(keep, a, b), jnp.where(keep, b, a)
        idxs[x], idxs[y] = jnp.where(keep, ia, ib), jnp.where(keep, ib, ia)
    for it in range(PEER_TOPK):
        m = jnp.max(vals[0], axis=0, keepdims=True)
        p = jnp.min(jnp.where(vals[0] == m, idxs[0], _BIG), axis=0, keepdims=True)
        val_ref[it:it + 1, :] = m
        pay_ref[it:it + 1, :] = p
        win = idxs[0] == p
        for k in range(PEER_TOPK - 1 - it):
            vals[k] = jnp.where(win, vals[k + 1], vals[k])
            idxs[k] = jnp.where(win, idxs[k + 1], idxs[k])


def _topk_kernel(q_ref, k1_ref, k2_ref, off_ref, gate_ref, v1, i1, v2, i2, tv, te, gall, *, tt):
    dn = (((1,), (1,)), ((), ()))
    a8 = lax.broadcasted_iota(I32, (SUBLANES, tt), 0)
    for h in range(PEER_HEADS):
        q1 = q_ref[:, 2 * h * PEER_HALF:(2 * h + 1) * PEER_HALF]
        q2 = q_ref[:, (2 * h + 1) * PEER_HALF:(2 * h + 2) * PEER_HALF]
        s1 = lax.dot_general(k1_ref[h], q1, dn, preferred_element_type=F32)
        s2 = lax.dot_general(k2_ref[h], q2, dn, preferred_element_type=F32)
        _extract_top_keys(s1, v1, i1)
        _extract_top_keys(s2, v2, i2)
        va, vb = v1[...], v2[...]
        ea, eb = i1[...] * float(PEER_N_KEYS), i2[...]
        lo8 = slice(0, SUBLANES)
        hi8 = slice(SUBLANES, 2 * SUBLANES)
        cv = [va[0:1] + vb, va[1:2] + vb[lo8], va[hi8] + vb[0:1]]
        ce = [ea[0:1] + eb, ea[1:2] + eb[lo8], ea[hi8] + eb[0:1]]
        for bcol, amax in ((0, 7), (1, 7), (2, 4), (3, 3), (4, 2)):
            ok = (a8 >= 2) & (a8 <= amax)
            cv.append(jnp.where(ok, va[lo8] + vb[bcol:bcol + 1], -jnp.inf))
            ce.append(ea[lo8] + eb[bcol:bcol + 1])
        _extract_top(jnp.concatenate(cv, axis=0), jnp.concatenate(ce, axis=0), tv, te)
        sc = tv[...]
        ex = jnp.exp(sc - jnp.max(sc, axis=0, keepdims=True))
        rows = slice(h * PEER_TOPK, (h + 1) * PEER_TOPK)
        gall[rows, :] = ex / jnp.sum(ex, axis=0, keepdims=True)
        off_ref[rows, :] = te[...].astype(I32) * WORD_ROWS
    gate_ref[...] = jnp.transpose(gall[...])


def _topk(q, k1, k2):
    t = q.shape[0]
    tt = min(TT_TOPK, t)
    kspec = pl.BlockSpec(k1.shape, lambda i: (0, 0, 0))
    small = lambda: pltpu.VMEM((PEER_TOPK, tt), F32)
    tok = pl.BlockSpec((tt, N_SEL), lambda i: (i, 0))
    return pl.pallas_call(
        functools.partial(_topk_kernel, tt=tt),
        grid=(t // tt,),
        in_specs=[pl.BlockSpec((tt, q.shape[1]), lambda i: (i, 0)), kspec, kspec],
        out_specs=[pl.BlockSpec((N_SEL, tt), lambda i: (0, i)), tok],
        out_shape=[jax.ShapeDtypeStruct((N_SEL, t), I32), jax.ShapeDtypeStruct((t, N_SEL), F32)],
        scratch_shapes=[small(), small(), small(), small(), small(), small(),
                        pltpu.VMEM((N_SEL, tt), F32)],
        compiler_params=_params(),
        name="topk",
    )(q, k1, k2)


def _off_copy(off_hbm, buf, sems, half_idx, s):
    return pltpu.make_async_copy(off_hbm.at[:, pl.ds(half_idx * PEER_HALF_TOK, PEER_HALF_TOK)], buf, sems.at[s])


def _run_halves(off_hbm, tab_hbm, off_a, off_b, tab_vmem, sems, run_half):
    i = pl.program_id(0)
    n = pl.num_programs(0)

    @pl.when(i == 0)
    def _():
        cp = pltpu.make_async_copy(tab_hbm, tab_vmem, sems.at[2])
        cp.start()
        _off_copy(off_hbm, off_a, sems, 0, 0).start()
        _off_copy(off_hbm, off_b, sems, 1, 1).start()
        cp.wait()

    for s, buf in enumerate((off_a, off_b)):
        _off_copy(off_hbm, buf, sems, 2 * i + s, s).wait()
        run_half(buf, s * PEER_HALF_TOK)

        @pl.when(i + 1 < n)
        def _():
            _off_copy(off_hbm, buf, sems, 2 * (i + 1) + s, s).start()


def _gather_token(off_smem, tab_vmem, col):
    rows = []
    for j in range(N_SEL):
        off = pl.multiple_of(off_smem.at[j][col], WORD_ROWS)
        rows.append(tab_vmem[pl.ds(off, WORD_ROWS), :])
    return pltpu.bitcast(jnp.concatenate(rows, axis=0), BF16)


def _slab_mask():
    s = lax.broadcasted_iota(I32, (SUBLANES, GATHER_ROWS), 0)
    n = lax.broadcasted_iota(I32, (SUBLANES, GATHER_ROWS), 1)
    return (n & (SUBLANES - 1)) == s


def _fold_matrix(shape, row_axis):
    n = lax.broadcasted_iota(I32, shape, row_axis)
    j = lax.broadcasted_iota(I32, shape, 1 - row_axis)
    return jnp.where(n // EXPERT_ROWS == j, 1.0, 0.0).astype(BF16)


def _merge_halves(a, b, m, k):
    r = pltpu.roll(b, k, axis=0)
    c = jnp.where(m, a, r)
    w = jnp.where(m, r, a)
    return c + pltpu.roll(w, SUBLANES - k, axis=0)


def _sublane_sums(xs, sub):
    m4, m2, m1 = sub < 4, (sub & 2) == 0, (sub & 1) == 0
    s = [_merge_halves(xs[a], xs[a + 4], m4, 4) for a in (0, 2, 1, 3)]
    return _merge_halves(_merge_halves(s[0], s[1], m2, 2), _merge_halves(s[2], s[3], m2, 2), m1, 1)


def _peer_u_kernel(off_hbm, x_ref, gate_ref, tab_hbm, w_ref, off_a, off_b, tab_vmem, cs_ref, sems):
    subw = lax.broadcasted_iota(I32, (SUBLANES, GATHER_ROWS), 0)
    mask = _slab_mask()
    dn = (((1,), (1,)), ((), ()))

    def run_half(off_smem, row_base):
        def group(g, carry):
            for oct_ in range(PEER_GROUP // SUBLANES):
                parts = []
                for tt in range(SUBLANES):
                    col = g * PEER_GROUP + oct_ * SUBLANES + tt
                    m = _gather_token(off_smem, tab_vmem, col)
                    r8 = pl.multiple_of((row_base + col) * SUBLANES, SUBLANES)
                    xh, xl = _split2(x_ref[pl.ds(r8, SUBLANES), :])
                    p = lax.dot_general(jnp.concatenate([xh, xl], axis=0), m, dn, preferred_element_type=F32)
                    parts.append(jnp.where(mask, p[:SUBLANES] + p[SUBLANES:], 0.0))
                r0 = pl.multiple_of(row_base + g * PEER_GROUP + oct_ * SUBLANES, SUBLANES)
                cs_ref[pl.ds(r0, SUBLANES), :] = _sublane_sums(parts, subw)
            return carry

        lax.fori_loop(0, PEER_HALF_TOK // PEER_GROUP, group, 0)

    _run_halves(off_hbm, tab_hbm, off_a, off_b, tab_vmem, sems, run_half)
    acts = _dot_exact_lhs(cs_ref[...], _fold_matrix((GATHER_ROWS, N_SEL), 0))
    w_ref[...] = gate_ref[...] * (0.5 * acts * (1.0 + lax.erf(acts * (2.0 ** -0.5))))


def _peer_scratch(tb, *extra):
    return [pltpu.SMEM((N_SEL, PEER_HALF_TOK), I32), pltpu.SMEM((N_SEL, PEER_HALF_TOK), I32),
            pltpu.VMEM((TAB_ROWS, LANES), I32), pltpu.VMEM((tb, GATHER_ROWS), F32), *extra,
            pltpu.SemaphoreType.DMA((3,))]


def _peer_u(off, x8, gate, tab):
    t = gate.shape[0]
    tb = TB_PEER
    tok = pl.BlockSpec((tb, N_SEL), lambda i: (i, 0))
    return pl.pallas_call(
        _peer_u_kernel,
        grid=(t // tb,),
        in_specs=[pl.BlockSpec(memory_space=pl.ANY),
                  pl.BlockSpec((tb * SUBLANES, LANES), lambda i: (i, 0)),
                  tok, pl.BlockSpec(memory_space=pl.ANY)],
        out_specs=tok,
        out_shape=jax.ShapeDtypeStruct((t, N_SEL), F32),
        scratch_shapes=_peer_scratch(tb),
        compiler_params=_params(),
        name="peer_u",
    )(off, x8, gate, tab)


def _peer_v_kernel(off_hbm, w_ref, h_ref, nw_ref, tab_hbm, out_ref, off_a, off_b, tab_vmem, wrep_ref, y_ref, sems):
    mask = _slab_mask()
    wrep_ref[...] = _dot_exact_lhs(w_ref[...], _fold_matrix((N_SEL, GATHER_ROWS), 1))

    def run_half(off_smem, row_base):
        def group(g, carry):
            for oct_ in range(PEER_GROUP // SUBLANES):
                r0 = pl.multiple_of(row_base + g * PEER_GROUP + oct_ * SUBLANES, SUBLANES)
                wrep = wrep_ref[pl.ds(r0, SUBLANES), :]
                for tt in range(SUBLANES):
                    col = g * PEER_GROUP + oct_ * SUBLANES + tt
                    m = _gather_token(off_smem, tab_vmem, col)
                    ah, al = _split2(jnp.where(mask, wrep[tt:tt + 1, :], 0.0))
                    res = jnp.dot(jnp.concatenate([ah, al], axis=0), m, preferred_element_type=F32)
                    r8 = pl.multiple_of((row_base + col) * SUBLANES, SUBLANES)
                    y = h_ref[pl.ds(r8, SUBLANES), :] + res[:SUBLANES] + res[SUBLANES:]
                    ss = jnp.sum(jnp.sum(y * y, axis=1, keepdims=True), axis=0, keepdims=True)
                    y_ref[pl.ds(r8, SUBLANES), :] = (
                        y * lax.rsqrt(ss * (1.0 / D_MODEL) + RMS_EPS) * nw_ref[...])
            return carry

        lax.fori_loop(0, PEER_HALF_TOK // PEER_GROUP, group, 0)

    _run_halves(off_hbm, tab_hbm, off_a, off_b, tab_vmem, sems, run_half)
    out_ref[...] = _from_slabs(y_ref[...])


def _peer_v(off, w, h8, nw8, tab):
    t = w.shape[0]
    tb = TB_PEER
    tok = pl.BlockSpec((tb, N_SEL), lambda i: (i, 0))
    return pl.pallas_call(
        _peer_v_kernel,
        grid=(t // tb,),
        in_specs=[pl.BlockSpec(memory_space=pl.ANY), tok,
                  pl.BlockSpec((tb * SUBLANES, LANES), lambda i: (i, 0)),
                  pl.BlockSpec((SUBLANES, LANES), lambda i: (0, 0)),
                  pl.BlockSpec(memory_space=pl.ANY)],
        out_specs=pl.BlockSpec((tb, D_MODEL), lambda i: (i, 0)),
        out_shape=jax.ShapeDtypeStruct((t, D_MODEL), F32),
        scratch_shapes=_peer_scratch(tb, pltpu.VMEM((tb * SUBLANES, LANES), F32)),
        compiler_params=_params(),
        name="peer_v",
    )(off, w, h8, nw8, tab)


def _table_kernel(w_ref, o_ref):
    o_ref[...] = pltpu.bitcast(_to_slabs(w_ref[...]).astype(BF16), I32)


def _expert_table(w):
    n = w.shape[0]
    eb = TABLE_BLOCK
    return pl.pallas_call(
        _table_kernel,
        grid=(n // eb,),
        in_specs=[pl.BlockSpec((eb, D_MODEL), lambda i: (i, 0))],
        out_specs=pl.BlockSpec((eb * WORD_ROWS, LANES), lambda i: (i, 0)),
        out_shape=jax.ShapeDtypeStruct((n * WORD_ROWS, LANES), I32),
        compiler_params=_params(),
        name="expert_table",
    )(w)


def kernel(x, norm_mix_w, w_in, conv_w, conv_b, b_igate, b_fgate, mlstm_norm_w, w_branch_sb, w_branch_ml, w_out, norm_ffn_w, peer_w_query, peer_keys1, peer_keys2, peer_u, peer_v, norm_final_w):
    batch, seq, d = x.shape
    t = batch * seq
    assert d == D_MODEL and w_in.shape[0] == 1
    assert seq % SBA_TQ == 0 and seq % ML_CHUNK == 0 and t % TM_PROJ == 0 and t % TT_TOPK == 0 and t % TB_PEER == 0
    xt = x.reshape(t, d)

    o_sb = 3 * SB_WIDTH
    o_qk = o_sb + 2 * ML_WIDTH
    o_v = o_qk + ML_WIDTH
    o_o = o_v + ML_WIDTH
    o_if = o_o + 2 * ML_HEADS
    w = w_in[0]
    wb = lambda a: a.astype(BF16)
    w_if = jnp.pad(w[:, o_o:o_if], ((0, 0), (0, LANES - 2 * ML_HEADS)))
    sb_qkv, ml_qk, ml_v, ml_o, gates, ml_if = _in_proj(
        xt, norm_mix_w[0].reshape(1, d), wb(w[:, :o_sb]), wb(w[:, o_sb:o_qk]), wb(w[:, o_qk:o_v]),
        wb(w[:, o_v:o_o]), wb(w[:, o_if:]), wb(w_if))

    y_sb = _sba(sb_qkv.reshape(batch, seq, 3 * SB_WIDTH), batch, seq)

    ifb = jnp.pad(jnp.concatenate([b_igate[0], b_fgate[0]]), (0, LANES - 2 * ML_HEADS)).reshape(1, LANES)
    y_ml = _mlstm(ml_qk.reshape(batch, seq, 2 * ML_WIDTH), ml_v.reshape(batch, seq, ML_WIDTH),
                  ml_o.reshape(batch, seq, ML_WIDTH), ml_if.reshape(batch, seq, LANES),
                  conv_w[0], conv_b[0].reshape(1, 2 * ML_WIDTH), ifb.astype(F32),
                  mlstm_norm_w[0].reshape(1, ML_WIDTH), batch, seq)

    h, hn, q = _merge(y_sb.reshape(t, SB_WIDTH), y_ml.reshape(t, ML_WIDTH), gates, xt,
                      wb(w_branch_sb[0]), wb(w_branch_ml[0]), wb(w_out[0]),
                      norm_ffn_w[0].reshape(1, d), wb(peer_w_query[0]))

    off, gate = _topk(q, wb(peer_keys1[0]), wb(peer_keys2[0]))

    wsel = _peer_u(off, hn, gate, _expert_table(peer_u[0]))

    nw8 = norm_final_w.reshape(SUBLANES, LANES)
    out = _peer_v(off, wsel, h, nw8, _expert_table(peer_v[0]))
    return out.reshape(batch, seq, d)
```

```python
import functools

import jax
import jax.numpy as jnp
from jax import lax
from jax.experimental import pallas as pl
from jax.experimental.pallas import tpu as pltpu

F32 = jnp.float32
BF16 = jnp.bfloat16
I32 = jnp.int32

D_MODEL = 1024
SB_HEADS = 8
SB_HEAD_DIM = 64
SB_WIDTH = SB_HEADS * SB_HEAD_DIM
ML_HEADS = 4
ML_HEAD_DIM = 128
ML_WIDTH = ML_HEADS * ML_HEAD_DIM
CONV_WIDTH = 4
PEER_HEADS = 8
PEER_N_KEYS = 128
PEER_N_EXPERTS = PEER_N_KEYS * PEER_N_KEYS
PEER_HALF = 128
PEER_TOPK = 16
N_SEL = PEER_HEADS * PEER_TOPK
RMS_EPS = 1e-6

LANES = 128
SUBLANES = 8
VMEM_LIMIT = 48 * 1024 * 1024

TM_PROJ = 256
SBA_TQ = 256
SBA_TK = 128
SBA_TK2 = 2 * SBA_TK
ML_CHUNK = 128
TT_TOPK = 256
PEER_HALF_TOK = 128
TB_PEER = 2 * PEER_HALF_TOK
PEER_GROUP = 64
EXPERT_ROWS = SUBLANES
WORD_ROWS = EXPERT_ROWS // 2
GATHER_ROWS = N_SEL * EXPERT_ROWS
TAB_ROWS = PEER_N_EXPERTS * WORD_ROWS
TABLE_BLOCK = 256


def _params():
    return pltpu.CompilerParams(dimension_semantics=None, vmem_limit_bytes=VMEM_LIMIT)


def _split3(a):
    hi = a.astype(BF16)
    r = a - hi.astype(F32)
    mid = r.astype(BF16)
    lo = (r - mid.astype(F32)).astype(BF16)
    return hi, mid, lo


def _split2(a):
    hi = a.astype(BF16)
    return hi, (a - hi.astype(F32)).astype(BF16)


def _dot_exact_rhs(a_bf16, b_f32):
    out = None
    for piece in _split3(b_f32):
        d = jnp.dot(a_bf16, piece, preferred_element_type=F32)
        out = d if out is None else out + d
    return out


def _dot_exact_lhs(a_f32, b_bf16):
    out = None
    for piece in _split3(a_f32):
        d = jnp.dot(piece, b_bf16, preferred_element_type=F32)
        out = d if out is None else out + d
    return out


def _inproj_kernel(x_ref, nw_ref, wsb, wqk, wv, wo, wg, wif, osb, oqk, ov, oo, og, oif):
    x = x_ref[...]
    ms = jnp.mean(x * x, axis=-1, keepdims=True)
    xb = (x * lax.rsqrt(ms + RMS_EPS) * nw_ref[...]).astype(BF16)
    osb[...] = jnp.dot(xb, wsb[...], preferred_element_type=F32).astype(BF16)
    oqk[...] = jnp.dot(xb, wqk[...], preferred_element_type=F32)
    ov[...] = jnp.dot(xb, wv[...], preferred_element_type=F32).astype(BF16)
    oo[...] = jnp.dot(xb, wo[...], preferred_element_type=F32)
    og[...] = jnp.dot(xb, wg[...], preferred_element_type=F32)
    oif[...] = jnp.dot(xb, wif[...], preferred_element_type=F32)


def _in_proj(xt, nw, wsb, wqk, wv, wo, wg, wif):
    t = xt.shape[0]
    tm = min(TM_PROJ, t)
    full = lambda a: pl.BlockSpec(a.shape, lambda i: (0, 0))
    tok = lambda n: pl.BlockSpec((tm, n), lambda i: (i, 0))
    ws = (wsb, wqk, wv, wo, wg, wif)
    outs = [(wsb.shape[1], BF16), (wqk.shape[1], F32), (wv.shape[1], BF16),
            (wo.shape[1], F32), (wg.shape[1], F32), (wif.shape[1], F32)]
    return pl.pallas_call(
        _inproj_kernel,
        grid=(t // tm,),
        in_specs=[tok(D_MODEL), full(nw)] + [full(w) for w in ws],
        out_specs=[tok(n) for n, _ in outs],
        out_shape=[jax.ShapeDtypeStruct((t, n), dt) for n, dt in outs],
        compiler_params=_params(),
        name="in_proj",
    )(xt, nw, *ws)


def _sba_step(qms, k_ref, v_ref, uu, carry_ref, acc_ref, kb2, row0, masked):
    c0 = pl.multiple_of(kb2 * SBA_TK2, SBA_TK2)
    kblk = k_ref[pl.ds(c0, SBA_TK2), :]
    vblk = v_ref[pl.ds(c0, SBA_TK2), :]
    if masked:
        rows = row0 + lax.broadcasted_iota(I32, (SBA_TQ, SBA_TK2), 0)
        cols = c0 + lax.broadcasted_iota(I32, (SBA_TQ, SBA_TK2), 1)
        causal = cols < rows
    for hh in range(2):
        z = lax.dot_general(qms[hh], kblk, (((1,), (1,)), ((), ())), preferred_element_type=F32)
        sp = jnp.maximum(z, 0.0) + jnp.log(1.0 + jnp.exp(-jnp.abs(z)))
        if masked:
            sp = jnp.where(causal, sp, 0.0)
        hi, lo = _split2(sp)
        su_lo = jnp.dot(jnp.concatenate([hi[:, :SBA_TK], lo[:, :SBA_TK]], axis=1), uu,
                        preferred_element_type=F32)
        su_hi = jnp.dot(jnp.concatenate([hi[:, SBA_TK:], lo[:, SBA_TK:]], axis=1), uu,
                        preferred_element_type=F32)
        carry = carry_ref[hh]
        later = jnp.concatenate([su_lo[:, :SBA_TK] + su_hi[:, SBA_TK:], su_hi[:, :SBA_TK]], axis=1)
        w = jnp.exp(z - sp - later - jnp.concatenate([carry, carry], axis=1))
        if masked:
            w = jnp.where(causal, w, 0.0)
        acc_ref[hh] += jnp.dot(w.astype(BF16), vblk, preferred_element_type=F32)
        carry_ref[hh] = carry + su_lo[:, SBA_TK:] + su_hi[:, SBA_TK:]


def _sba_kernel(q_ref, k_ref, v_ref, y_ref, carry_ref, acc_ref, *, seq):
    r = lax.broadcasted_iota(I32, (SBA_TK2, SBA_TK2), 0) % SBA_TK
    c = lax.broadcasted_iota(I32, (SBA_TK2, SBA_TK2), 1)
    uu = jnp.where((c >= SBA_TK) | (r > c), 1.0, 0.0).astype(BF16)
    lane = lax.broadcasted_iota(I32, (SBA_TQ, LANES), 1)
    assert SBA_TQ == SBA_TK2

    def qblock(qi, carry):
        row0 = pl.multiple_of(qi * SBA_TQ, SBA_TQ)
        qf = q_ref[pl.ds(row0, SBA_TQ), :] * jnp.asarray(SB_HEAD_DIM ** -0.5, BF16)
        zero = jnp.zeros_like(qf)
        qms = [jnp.where(lane < SB_HEAD_DIM, qf, zero), jnp.where(lane >= SB_HEAD_DIM, qf, zero)]
        carry_ref[...] = jnp.zeros_like(carry_ref)
        acc_ref[...] = jnp.zeros_like(acc_ref)
        _sba_step(qms, k_ref, v_ref, uu, carry_ref, acc_ref, qi, row0, True)

        def below(kk, c2):
            for d in range(2):
                _sba_step(qms, k_ref, v_ref, uu, carry_ref, acc_ref, qi - 1 - 2 * kk - d, row0, False)
            return c2

        lax.fori_loop(0, qi // 2, below, 0)

        @pl.when(qi % 2 == 1)
        def _():
            _sba_step(qms, k_ref, v_ref, uu, carry_ref, acc_ref, 0, row0, False)

        y = jnp.where(lane < SB_HEAD_DIM, acc_ref[0], acc_ref[1])
        y_ref[pl.ds(row0, SBA_TQ), :] = y.astype(BF16)
        return carry

    lax.fori_loop(0, seq // SBA_TQ, qblock, 0)


def _sba(sb_qkv, batch, seq):
    npair = SB_WIDTH // LANES
    blk = lambda off: pl.BlockSpec((None, seq, LANES), lambda b, p: (b, 0, off + p))
    return pl.pallas_call(
        functools.partial(_sba_kernel, seq=seq),
        grid=(batch, npair),
        in_specs=[blk(0), blk(npair), blk(2 * npair)],
        out_specs=pl.BlockSpec((None, seq, LANES), lambda b, p: (b, 0, p)),
        out_shape=jax.ShapeDtypeStruct((batch, seq, SB_WIDTH), BF16),
        scratch_shapes=[pltpu.VMEM((2, SBA_TQ, SBA_TK), F32), pltpu.VMEM((2, SBA_TQ, LANES), F32)],
        compiler_params=_params(),
        name="sba",
    )(sb_qkv, sb_qkv, sb_qkv)


def _conv_silu(x, w, b, row):
    y = x * w[CONV_WIDTH - 1:CONV_WIDTH, :] + b
    for k in range(1, CONV_WIDTH):
        shifted = jnp.where(row >= k, pltpu.roll(x, k, axis=0), 0.0)
        y = y + shifted * w[CONV_WIDTH - 1 - k:CONV_WIDTH - k, :]
    return y * jax.nn.sigmoid(y)


def _log_sigmoid(x):
    return -(jnp.maximum(-x, 0.0) + jnp.log1p(jnp.exp(-jnp.abs(x))))


def _mlstm_kernel(qk_ref, v_ref, o_ref, if_ref, cw_ref, cb_ref, ifb_ref, nw_ref, y_ref,
                  qs_ref, ks_ref, c_ref, n_ref, m_ref, *, seq):
    L = ML_CHUNK
    row = lax.broadcasted_iota(I32, (seq, LANES), 0)

    def conv_head(hd, carry):
        cq = pl.multiple_of(hd * LANES, LANES)
        ck = pl.multiple_of(ML_WIDTH + hd * LANES, LANES)
        q = _conv_silu(qk_ref[:, pl.ds(cq, LANES)], cw_ref[:, pl.ds(cq, LANES)], cb_ref[:, pl.ds(cq, LANES)], row)
        k = _conv_silu(qk_ref[:, pl.ds(ck, LANES)], cw_ref[:, pl.ds(ck, LANES)], cb_ref[:, pl.ds(ck, LANES)], row)
        qs_ref[hd] = q.astype(BF16)
        ks_ref[hd] = (k * (ML_HEAD_DIM ** -0.5)).astype(BF16)
        return carry

    lax.fori_loop(0, ML_HEADS, conv_head, 0)

    src = lax.broadcasted_iota(I32, (LANES, LANES), 0)
    rr = lax.broadcasted_iota(I32, (L, L), 0)
    cc = lax.broadcasted_iota(I32, (L, L), 1)
    tril = cc <= rr
    tril_b = jnp.where(tril, 1.0, 0.0).astype(BF16)
    eye = jnp.where(cc == rr, 1.0, 0.0)
    ones_b = jnp.ones((L, L), BF16)

    c_ref[...] = jnp.zeros_like(c_ref)
    n_ref[...] = jnp.zeros_like(n_ref)
    m_ref[...] = jnp.zeros_like(m_ref)

    def chunk(ci, carry):
        r0 = pl.multiple_of(ci * L, L)
        gates = if_ref[pl.ds(r0, L), :] + ifb_ref[...]
        log_f = _log_sigmoid(gates)
        for hd in range(ML_HEADS):
            lanes = slice(hd * LANES, (hd + 1) * LANES)
            ic = _dot_exact_lhs(gates, jnp.where(src == hd, 1.0, 0.0).astype(BF16))
            lf = _dot_exact_lhs(log_f, jnp.where(src == hd + ML_HEADS, 1.0, 0.0).astype(BF16))
            qc = qs_ref[hd, pl.ds(r0, L), :]
            kc = ks_ref[hd, pl.ds(r0, L), :]
            vc = v_ref[pl.ds(r0, L), lanes]
            m = m_ref[hd]
            n = n_ref[hd]
            cmat = c_ref[hd]
            b = _dot_exact_rhs(tril_b, lf)
            drow = _dot_exact_rhs(ones_b, eye * (ic - b))
            dmat = jnp.where(tril, b + drow, -jnp.inf)
            inter = b + m
            m_t = jnp.maximum(inter, jnp.max(dmat, axis=-1, keepdims=True))
            w_intra = jnp.exp(dmat - m_t)
            w_inter = jnp.exp(inter - m_t)
            s = lax.dot_general(qc, kc, (((1,), (1,)), ((), ())), preferred_element_type=F32) * w_intra
            num = jnp.dot(s.astype(BF16), vc, preferred_element_type=F32)
            num = num + w_inter * jnp.dot(qc, cmat.astype(BF16), preferred_element_type=F32)
            qn = jnp.sum(qc.astype(F32) * n, axis=-1, keepdims=True)
            den = jnp.sum(s, axis=-1, keepdims=True) + w_inter * qn
            h = num / jnp.maximum(jnp.abs(den), jnp.exp(-m_t))
            hn = h * lax.rsqrt(jnp.mean(h * h, axis=-1, keepdims=True) + RMS_EPS) * nw_ref[:, lanes]
            y_ref[pl.ds(r0, L), lanes] = (jax.nn.sigmoid(o_ref[pl.ds(r0, L), lanes]) * hn).astype(BF16)
            b_last = b[L - 1:L, :]
            g = b_last - b + ic
            m_new = jnp.maximum(b_last + m, jnp.max(g, axis=0, keepdims=True))
            decay = jnp.exp(b_last + m - m_new)
            kw = kc.astype(F32) * jnp.exp(g - m_new)
            c_ref[hd] = decay * cmat + lax.dot_general(
                kw.astype(BF16), vc, (((0,), (0,)), ((), ())), preferred_element_type=F32)
            n_ref[hd] = decay * n + jnp.sum(kw, axis=0, keepdims=True)
            m_ref[hd] = m_new
        return carry

    lax.fori_loop(0, seq // L, chunk, 0)


def _mlstm(ml_qk, ml_v, ml_o, ml_if, conv_w, conv_b, ifb, nw, batch, seq):
    rows = lambda n: pl.BlockSpec((None, seq, n), lambda b: (b, 0, 0))
    full = lambda a: pl.BlockSpec(a.shape, lambda b: (0, 0))
    return pl.pallas_call(
        functools.partial(_mlstm_kernel, seq=seq),
        grid=(batch,),
        in_specs=[rows(2 * ML_WIDTH), rows(ML_WIDTH), rows(ML_WIDTH), rows(LANES),
                  full(conv_w), full(conv_b), full(ifb), full(nw)],
        out_specs=rows(ML_WIDTH),
        out_shape=jax.ShapeDtypeStruct((batch, seq, ML_WIDTH), BF16),
        scratch_shapes=[pltpu.VMEM((ML_HEADS, seq, LANES), BF16), pltpu.VMEM((ML_HEADS, seq, LANES), BF16),
                        pltpu.VMEM((ML_HEADS, LANES, LANES), F32),
                        pltpu.VMEM((ML_HEADS, 1, LANES), F32), pltpu.VMEM((ML_HEADS, 1, LANES), F32)],
        compiler_params=_params(),
        name="mlstm",
    )(ml_qk, ml_v, ml_o, ml_if, conv_w, conv_b, ifb, nw)


def _to_slabs(a):
    t = a.shape[0]
    parts = [a[:, LANES * r:LANES * (r + 1)] for r in range(SUBLANES)]
    return jnp.stack(parts, axis=1).reshape(t * SUBLANES, LANES)


def _from_slabs(a):
    t = a.shape[0] // SUBLANES
    y = a.reshape(t, SUBLANES, LANES)
    return jnp.concatenate([y[:, r, :] for r in range(SUBLANES)], axis=1)


def _merge_kernel(ysb, yml, g_ref, x_ref, wsb, wml, wout, nw_ref, wq, h_ref, hn_ref, q_ref):
    a = jnp.dot(ysb[...], wsb[...], preferred_element_type=F32)
    b = jnp.dot(yml[...], wml[...], preferred_element_type=F32)
    g = g_ref[...]
    merged = jax.nn.sigmoid(g[:, :D_MODEL]) * a + jax.nn.sigmoid(g[:, D_MODEL:]) * b
    h = x_ref[...] + jnp.dot(merged.astype(BF16), wout[...], preferred_element_type=F32)
    h_ref[...] = _to_slabs(h)
    hn = h * lax.rsqrt(jnp.mean(h * h, axis=-1, keepdims=True) + RMS_EPS) * nw_ref[...]
    hn_ref[...] = _to_slabs(hn)
    q_ref[...] = jnp.dot(hn.astype(BF16), wq[...], preferred_element_type=F32).astype(BF16)


def _merge(ysb, yml, gates, xt, wsb, wml, wout, nw, wq):
    t = xt.shape[0]
    tm = min(TM_PROJ, t)
    full = lambda a: pl.BlockSpec(a.shape, lambda i: (0, 0))
    tok = lambda n: pl.BlockSpec((tm, n), lambda i: (i, 0))
    slab = pl.BlockSpec((tm * SUBLANES, LANES), lambda i: (i, 0))
    nq = wq.shape[1]
    return pl.pallas_call(
        _merge_kernel,
        grid=(t // tm,),
        in_specs=[tok(SB_WIDTH), tok(ML_WIDTH), tok(2 * D_MODEL), tok(D_MODEL),
                  full(wsb), full(wml), full(wout), full(nw), full(wq)],
        out_specs=[slab, slab, tok(nq)],
        out_shape=[jax.ShapeDtypeStruct((t * SUBLANES, LANES), F32),
                   jax.ShapeDtypeStruct((t * SUBLANES, LANES), F32),
                   jax.ShapeDtypeStruct((t, nq), BF16)],
        compiler_params=_params(),
        name="merge",
    )(ysb, yml, gates, xt, wsb, wml, wout, nw, wq)


_BIG = 3.0e38


def _extract_top(vals, payload, val_ref, pay_ref):
    for it in range(PEER_TOPK):
        m = jnp.max(vals, axis=0, keepdims=True)
        p = jnp.min(jnp.where(vals == m, payload, _BIG), axis=0, keepdims=True)
        val_ref[it:it + 1, :] = m
        pay_ref[it:it + 1, :] = p
        vals = jnp.where(payload == p, -jnp.inf, vals)


def _batcher_pairs(n):
    pairs = []
    p = 1
    while p < n:
        k = p
        while k >= 1:
            for j in range(k % p, n - k, 2 * k):
                for i in range(min(k, n - j - k)):
                    if (i + j) // (2 * p) == (i + j + k) // (2 * p):
                        pairs.append((i + j, i + j + k))
            k //= 2
        p *= 2
    return pairs


def _extract_top_keys(s, val_ref, pay_ref):
    nv = PEER_N_KEYS // SUBLANES
    sub = lax.broadcasted_iota(I32, (SUBLANES, s.shape[1]), 0).astype(F32)
    vals = [s[SUBLANES * k:SUBLANES * (k + 1), :] for k in range(nv)]
    idxs = [sub + float(SUBLANES * k) for k in range(nv)]
    for x, y in _batcher_pairs(nv):
        a, b, ia, ib = vals[x], vals[y], idxs[x], idxs[y]
        keep = a >= b
        vals[x], vals[y] = jnp.where(keep, a, b), jnp.where(keep, b, a)
        idxs[x], idxs[y] = jnp.where(keep, ia, ib), jnp.where(keep, ib, ia)
    for it in range(PEER_TOPK):
        m = jnp.max(vals[0], axis=0, keepdims=True)
        p = jnp.min(jnp.where(vals[0] == m, idxs[0], _BIG), axis=0, keepdims=True)
        val_ref[it:it + 1, :] = m
        pay_ref[it:it + 1, :] = p
        win = idxs[0] == p
        for k in range(PEER_TOPK - 1 - it):
            vals[k] = jnp.where(win, vals[k + 1], vals[k])
            idxs[k] = jnp.where(win, idxs[k + 1], idxs[k])


def _topk_kernel(q_ref, k1_ref, k2_ref, off_ref, gate_ref, v1, i1, v2, i2, tv, te, gall, *, tt):
    dn = (((1,), (1,)), ((), ()))
    a8 = lax.broadcasted_iota(I32, (SUBLANES, tt), 0)
    for h in range(PEER_HEADS):
        q1 = q_ref[:, 2 * h * PEER_HALF:(2 * h + 1) * PEER_HALF]
        q2 = q_ref[:, (2 * h + 1) * PEER_HALF:(2 * h + 2) * PEER_HALF]
        s1 = lax.dot_general(k1_ref[h], q1, dn, preferred_element_type=F32)
        s2 = lax.dot_general(k2_ref[h], q2, dn, preferred_element_type=F32)
        _extract_top_keys(s1, v1, i1)
        _extract_top_keys(s2, v2, i2)
        va, vb = v1[...], v2[...]
        ea, eb = i1[...] * float(PEER_N_KEYS), i2[...]
        lo8 = slice(0, SUBLANES)
        hi8 = slice(SUBLANES, 2 * SUBLANES)
        cv = [va[0:1] + vb, va[1:2] + vb[lo8], va[hi8] + vb[0:1]]
        ce = [ea[0:1] + eb, ea[1:2] + eb[lo8], ea[hi8] + eb[0:1]]
        for bcol, amax in ((0, 7), (1, 7), (2, 4), (3, 3), (4, 2)):
            ok = (a8 >= 2) & (a8 <= amax)
            cv.append(jnp.where(ok, va[lo8] + vb[bcol:bcol + 1], -jnp.inf))
            ce.append(ea[lo8] + eb[bcol:bcol + 1])
        _extract_top(jnp.concatenate(cv, axis=0), jnp.concatenate(ce, axis=0), tv, te)
        sc = tv[...]
        ex = jnp.exp(sc - jnp.max(sc, axis=0, keepdims=True))
        rows = slice(h * PEER_TOPK, (h + 1) * PEER_TOPK)
        gall[rows, :] = ex / jnp.sum(ex, axis=0, keepdims=True)
        off_ref[rows, :] = te[...].astype(I32) * WORD_ROWS
    gate_ref[...] = jnp.transpose(gall[...])


def _topk(q, k1, k2):
    t = q.shape[0]
    tt = min(TT_TOPK, t)
    kspec = pl.BlockSpec(k1.shape, lambda i: (0, 0, 0))
    small = lambda: pltpu.VMEM((PEER_TOPK, tt), F32)
    tok = pl.BlockSpec((tt, N_SEL), lambda i: (i, 0))
    return pl.pallas_call(
        functools.partial(_topk_kernel, tt=tt),
        grid=(t // tt,),
        in_specs=[pl.BlockSpec((tt, q.shape[1]), lambda i: (i, 0)), kspec, kspec],
        out_specs=[pl.BlockSpec((N_SEL, tt), lambda i: (0, i)), tok],
        out_shape=[jax.ShapeDtypeStruct((N_SEL, t), I32), jax.ShapeDtypeStruct((t, N_SEL), F32)],
        scratch_shapes=[small(), small(), small(), small(), small(), small(),
                        pltpu.VMEM((N_SEL, tt), F32)],
        compiler_params=_params(),
        name="topk",
    )(q, k1, k2)


def _off_copy(off_hbm, buf, sems, half_idx, s):
    return pltpu.make_async_copy(off_hbm.at[:, pl.ds(half_idx * PEER_HALF_TOK, PEER_HALF_TOK)], buf, sems.at[s])


def _run_halves(off_hbm, tab_hbm, off_a, off_b, tab_vmem, sems, run_half):
    i = pl.program_id(0)
    n = pl.num_programs(0)

    @pl.when(i == 0)
    def _():
        cp = pltpu.make_async_copy(tab_hbm, tab_vmem, sems.at[2])
        cp.start()
        _off_copy(off_hbm, off_a, sems, 0, 0).start()
        _off_copy(off_hbm, off_b, sems, 1, 1).start()
        cp.wait()

    for s, buf in enumerate((off_a, off_b)):
        _off_copy(off_hbm, buf, sems, 2 * i + s, s).wait()
        run_half(buf, s * PEER_HALF_TOK)

        @pl.when(i + 1 < n)
        def _():
            _off_copy(off_hbm, buf, sems, 2 * (i + 1) + s, s).start()


def _gather_token(off_smem, tab_vmem, col):
    rows = []
    for j in range(N_SEL):
        off = pl.multiple_of(off_smem.at[j][col], WORD_ROWS)
        rows.append(tab_vmem[pl.ds(off, WORD_ROWS), :])
    return pltpu.bitcast(jnp.concatenate(rows, axis=0), BF16)


def _slab_mask():
    s = lax.broadcasted_iota(I32, (SUBLANES, GATHER_ROWS), 0)
    n = lax.broadcasted_iota(I32, (SUBLANES, GATHER_ROWS), 1)
    return (n & (SUBLANES - 1)) == s


def _fold_matrix(shape, row_axis):
    n = lax.broadcasted_iota(I32, shape, row_axis)
    j = lax.broadcasted_iota(I32, shape, 1 - row_axis)
    return jnp.where(n // EXPERT_ROWS == j, 1.0, 0.0).astype(BF16)


def _merge_halves(a, b, m, k):
    r = pltpu.roll(b, k, axis=0)
    c = jnp.where(m, a, r)
    w = jnp.where(m, r, a)
    return c + pltpu.roll(w, SUBLANES - k, axis=0)


def _sublane_sums(xs, sub):
    m4, m2, m1 = sub < 4, (sub & 2) == 0, (sub & 1) == 0
    s = [_merge_halves(xs[a], xs[a + 4], m4, 4) for a in (0, 2, 1, 3)]
    return _merge_halves(_merge_halves(s[0], s[1], m2, 2), _merge_halves(s[2], s[3], m2, 2), m1, 1)


def _peer_u_kernel(off_hbm, x_ref, gate_ref, tab_hbm, w_ref, off_a, off_b, tab_vmem, cs_ref, sems):
    subw = lax.broadcasted_iota(I32, (SUBLANES, GATHER_ROWS), 0)
    mask = _slab_mask()
    dn = (((1,), (1,)), ((), ()))

    def run_half(off_smem, row_base):
        def group(g, carry):
            for oct_ in range(PEER_GROUP // SUBLANES):
                parts = []
                for tt in range(SUBLANES):
                    col = g * PEER_GROUP + oct_ * SUBLANES + tt
                    m = _gather_token(off_smem, tab_vmem, col)
                    r8 = pl.multiple_of((row_base + col) * SUBLANES, SUBLANES)
                    xh, xl = _split2(x_ref[pl.ds(r8, SUBLANES), :])
                    p = lax.dot_general(jnp.concatenate([xh, xl], axis=0), m, dn, preferred_element_type=F32)
                    parts.append(jnp.where(mask, p[:SUBLANES] + p[SUBLANES:], 0.0))
                r0 = pl.multiple_of(row_base + g * PEER_GROUP + oct_ * SUBLANES, SUBLANES)
                cs_ref[pl.ds(r0, SUBLANES), :] = _sublane_sums(parts, subw)
            return carry

        lax.fori_loop(0, PEER_HALF_TOK // PEER_GROUP, group, 0)

    _run_halves(off_hbm, tab_hbm, off_a, off_b, tab_vmem, sems, run_half)
    acts = _dot_exact_lhs(cs_ref[...], _fold_matrix((GATHER_ROWS, N_SEL), 0))
    w_ref[...] = gate_ref[...] * (0.5 * acts * (1.0 + lax.erf(acts * (2.0 ** -0.5))))


def _peer_scratch(tb, *extra):
    return [pltpu.SMEM((N_SEL, PEER_HALF_TOK), I32), pltpu.SMEM((N_SEL, PEER_HALF_TOK), I32),
            pltpu.VMEM((TAB_ROWS, LANES), I32), pltpu.VMEM((tb, GATHER_ROWS), F32), *extra,
            pltpu.SemaphoreType.DMA((3,))]


def _peer_u(off, x8, gate, tab):
    t = gate.shape[0]
    tb = TB_PEER
    tok = pl.BlockSpec((tb, N_SEL), lambda i: (i, 0))
    return pl.pallas_call(
        _peer_u_kernel,
        grid=(t // tb,),
        in_specs=[pl.BlockSpec(memory_space=pl.ANY),
                  pl.BlockSpec((tb * SUBLANES, LANES), lambda i: (i, 0)),
                  tok, pl.BlockSpec(memory_space=pl.ANY)],
        out_specs=tok,
        out_shape=jax.ShapeDtypeStruct((t, N_SEL), F32),
        scratch_shapes=_peer_scratch(tb),
        compiler_params=_params(),
        name="peer_u",
    )(off, x8, gate, tab)


def _peer_v_kernel(off_hbm, w_ref, h_ref, nw_ref, tab_hbm, out_ref, off_a, off_b, tab_vmem, wrep_ref, y_ref, sems):
    mask = _slab_mask()
    wrep_ref[...] = _dot_exact_lhs(w_ref[...], _fold_matrix((N_SEL, GATHER_ROWS), 1))

    def run_half(off_smem, row_base):
        def group(g, carry):
            for oct_ in range(PEER_GROUP // SUBLANES):
                r0 = pl.multiple_of(row_base + g * PEER_GROUP + oct_ * SUBLANES, SUBLANES)
                wrep = wrep_ref[pl.ds(r0, SUBLANES), :]
                for tt in range(SUBLANES):
                    col = g * PEER_GROUP + oct_ * SUBLANES + tt
                    m = _gather_token(off_smem, tab_vmem, col)
                    ah, al = _split2(jnp.where(mask, wrep[tt:tt + 1, :], 0.0))
                    res = jnp.dot(jnp.concatenate([ah, al], axis=0), m, preferred_element_type=F32)
                    r8 = pl.multiple_of((row_base + col) * SUBLANES, SUBLANES)
                    y = h_ref[pl.ds(r8, SUBLANES), :] + res[:SUBLANES] + res[SUBLANES:]
                    ss = jnp.sum(jnp.sum(y * y, axis=1, keepdims=True), axis=0, keepdims=True)
                    y_ref[pl.ds(r8, SUBLANES), :] = (
                        y * lax.rsqrt(ss * (1.0 / D_MODEL) + RMS_EPS) * nw_ref[...])
            return carry

        lax.fori_loop(0, PEER_HALF_TOK // PEER_GROUP, group, 0)

    _run_halves(off_hbm, tab_hbm, off_a, off_b, tab_vmem, sems, run_half)
    out_ref[...] = _from_slabs(y_ref[...])


def _peer_v(off, w, h8, nw8, tab):
    t = w.shape[0]
    tb = TB_PEER
    tok = pl.BlockSpec((tb, N_SEL), lambda i: (i, 0))
    return pl.pallas_call(
        _peer_v_kernel,
        grid=(t // tb,),
        in_specs=[pl.BlockSpec(memory_space=pl.ANY), tok,
                  pl.BlockSpec((tb * SUBLANES, LANES), lambda i: (i, 0)),
                  pl.BlockSpec((SUBLANES, LANES), lambda i: (0, 0)),
                  pl.BlockSpec(memory_space=pl.ANY)],
        out_specs=pl.BlockSpec((tb, D_MODEL), lambda i: (i, 0)),
        out_shape=jax.ShapeDtypeStruct((t, D_MODEL), F32),
        scratch_shapes=_peer_scratch(tb, pltpu.VMEM((tb * SUBLANES, LANES), F32)),
        compiler_params=_params(),
        name="peer_v",
    )(off, w, h8, nw8, tab)


def _table_kernel(w_ref, o_ref):
    o_ref[...] = pltpu.bitcast(_to_slabs(w_ref[...]).astype(BF16), I32)


def _expert_table(w):
    n = w.shape[0]
    eb = TABLE_BLOCK
    return pl.pallas_call(
        _table_kernel,
        grid=(n // eb,),
        in_specs=[pl.BlockSpec((eb, D_MODEL), lambda i: (i, 0))],
        out_specs=pl.BlockSpec((eb * WORD_ROWS, LANES), lambda i: (i, 0)),
        out_shape=jax.ShapeDtypeStruct((n * WORD_ROWS, LANES), I32),
        compiler_params=_params(),
        name="expert_table",
    )(w)


def kernel(x, norm_mix_w, w_in, conv_w, conv_b, b_igate, b_fgate, mlstm_norm_w, w_branch_sb, w_branch_ml, w_out, norm_ffn_w, peer_w_query, peer_keys1, peer_keys2, peer_u, peer_v, norm_final_w):
    batch, seq, d = x.shape
    t = batch * seq
    assert d == D_MODEL and w_in.shape[0] == 1
    assert seq % SBA_TQ == 0 and seq % ML_CHUNK == 0 and t % TM_PROJ == 0 and t % TT_TOPK == 0 and t % TB_PEER == 0
    xt = x.reshape(t, d)

    o_sb = 3 * SB_WIDTH
    o_qk = o_sb + 2 * ML_WIDTH
    o_v = o_qk + ML_WIDTH
    o_o = o_v + ML_WIDTH
    o_if = o_o + 2 * ML_HEADS
    w = w_in[0]
    wb = lambda a: a.astype(BF16)
    w_if = jnp.pad(w[:, o_o:o_if], ((0, 0), (0, LANES - 2 * ML_HEADS)))
    sb_qkv, ml_qk, ml_v, ml_o, gates, ml_if = _in_proj(
        xt, norm_mix_w[0].reshape(1, d), wb(w[:, :o_sb]), wb(w[:, o_sb:o_qk]), wb(w[:, o_qk:o_v]),
        wb(w[:, o_v:o_o]), wb(w[:, o_if:]), wb(w_if))

    y_sb = _sba(sb_qkv.reshape(batch, seq, 3 * SB_WIDTH), batch, seq)

    ifb = jnp.pad(jnp.concatenate([b_igate[0], b_fgate[0]]), (0, LANES - 2 * ML_HEADS)).reshape(1, LANES)
    y_ml = _mlstm(ml_qk.reshape(batch, seq, 2 * ML_WIDTH), ml_v.reshape(batch, seq, ML_WIDTH),
                  ml_o.reshape(batch, seq, ML_WIDTH), ml_if.reshape(batch, seq, LANES),
                  conv_w[0], conv_b[0].reshape(1, 2 * ML_WIDTH), ifb.astype(F32),
                  mlstm_norm_w[0].reshape(1, ML_WIDTH), batch, seq)

    h, hn, q = _merge(y_sb.reshape(t, SB_WIDTH), y_ml.reshape(t, ML_WIDTH), gates, xt,
                      wb(w_branch_sb[0]), wb(w_branch_ml[0]), wb(w_out[0]),
                      norm_ffn_w[0].reshape(1, d), wb(peer_w_query[0]))

    off, gate = _topk(q, wb(peer_keys1[0]), wb(peer_keys2[0]))

    wsel = _peer_u(off, hn, gate, _expert_table(peer_u[0]))

    nw8 = norm_final_w.reshape(SUBLANES, LANES)
    out = _peer_v(off, wsel, h, nw8, _expert_table(peer_v[0]))
    return out.reshape(batch, seq, d)
```

```python
import functools

import jax
import jax.numpy as jnp
from jax import lax
from jax.experimental import pallas as pl
from jax.experimental.pallas import tpu as pltpu

F32 = jnp.float32
BF16 = jnp.bfloat16
I32 = jnp.int32

D_MODEL = 1024
SB_HEADS = 8
SB_HEAD_DIM = 64
SB_WIDTH = SB_HEADS * SB_HEAD_DIM
ML_HEADS = 4
ML_HEAD_DIM = 128
ML_WIDTH = ML_HEADS * ML_HEAD_DIM
CONV_WIDTH = 4
PEER_HEADS = 8
PEER_N_KEYS = 128
PEER_N_EXPERTS = PEER_N_KEYS * PEER_N_KEYS
PEER_HALF = 128
PEER_TOPK = 16
N_SEL = PEER_HEADS * PEER_TOPK
RMS_EPS = 1e-6

LANES = 128
SUBLANES = 8
VMEM_LIMIT = 48 * 1024 * 1024

TM_PROJ = 256
SBA_TQ = 256
SBA_TK = 128
SBA_TK2 = 2 * SBA_TK
ML_CHUNK = 128
TT_TOPK = 256
PEER_HALF_TOK = 128
TB_PEER = 2 * PEER_HALF_TOK
PEER_GROUP = 64
EXPERT_ROWS = SUBLANES
WORD_ROWS = EXPERT_ROWS // 2
GATHER_ROWS = N_SEL * EXPERT_ROWS
TAB_ROWS = PEER_N_EXPERTS * WORD_ROWS
TABLE_BLOCK = 256


def _params():
    return pltpu.CompilerParams(dimension_semantics=None, vmem_limit_bytes=VMEM_LIMIT)


def _split3(a):
    hi = a.astype(BF16)
    r = a - hi.astype(F32)
    mid = r.astype(BF16)
    lo = (r - mid.astype(F32)).astype(BF16)
    return hi, mid, lo


def _split2(a):
    hi = a.astype(BF16)
    return hi, (a - hi.astype(F32)).astype(BF16)


def _dot_exact_rhs(a_bf16, b_f32):
    out = None
    for piece in _split3(b_f32):
        d = jnp.dot(a_bf16, piece, preferred_element_type=F32)
        out = d if out is None else out + d
    return out


def _dot_exact_lhs(a_f32, b_bf16):
    out = None
    for piece in _split3(a_f32):
        d = jnp.dot(piece, b_bf16, preferred_element_type=F32)
        out = d if out is None else out + d
    return out


def _inproj_kernel(x_ref, nw_ref, wsb, wqk, wv, wo, wg, wif, osb, oqk, ov, oo, og, oif):
    x = x_ref[...]
    ms = jnp.mean(x * x, axis=-1, keepdims=True)
    xb = (x * lax.rsqrt(ms + RMS_EPS) * nw_ref[...]).astype(BF16)
    osb[...] = jnp.dot(xb, wsb[...], preferred_element_type=F32).astype(BF16)
    oqk[...] = jnp.dot(xb, wqk[...], preferred_element_type=F32)
    ov[...] = jnp.dot(xb, wv[...], preferred_element_type=F32).astype(BF16)
    oo[...] = jnp.dot(xb, wo[...], preferred_element_type=F32)
    og[...] = jnp.dot(xb, wg[...], preferred_element_type=F32)
    oif[...] = jnp.dot(xb, wif[...], preferred_element_type=F32)


def _in_proj(xt, nw, wsb, wqk, wv, wo, wg, wif):
    t = xt.shape[0]
    tm = min(TM_PROJ, t)
    full = lambda a: pl.BlockSpec(a.shape, lambda i: (0, 0))
    tok = lambda n: pl.BlockSpec((tm, n), lambda i: (i, 0))
    ws = (wsb, wqk, wv, wo, wg, wif)
    outs = [(wsb.shape[1], BF16), (wqk.shape[1], F32), (wv.shape[1], BF16),
            (wo.shape[1], F32), (wg.shape[1], F32), (wif.shape[1], F32)]
    return pl.pallas_call(
        _inproj_kernel,
        grid=(t // tm,),
        in_specs=[tok(D_MODEL), full(nw)] + [full(w) for w in ws],
        out_specs=[tok(n) for n, _ in outs],
        out_shape=[jax.ShapeDtypeStruct((t, n), dt) for n, dt in outs],
        compiler_params=_params(),
        name="in_proj",
    )(xt, nw, *ws)


def _sba_step(qms, k_ref, v_ref, uu, carry_ref, acc_ref, kb2, row0, masked):
    c0 = pl.multiple_of(kb2 * SBA_TK2, SBA_TK2)
    kblk = k_ref[pl.ds(c0, SBA_TK2), :]
    vblk = v_ref[pl.ds(c0, SBA_TK2), :]
    if masked:
        rows = row0 + lax.broadcasted_iota(I32, (SBA_TQ, SBA_TK2), 0)
        cols = c0 + lax.broadcasted_iota(I32, (SBA_TQ, SBA_TK2), 1)
        causal = cols < rows
    for hh in range(2):
        z = lax.dot_general(qms[hh], kblk, (((1,), (1,)), ((), ())), preferred_element_type=F32)
        sp = jnp.maximum(z, 0.0) + jnp.log(1.0 + jnp.exp(-jnp.abs(z)))
        if masked:
            sp = jnp.where(causal, sp, 0.0)
        hi, lo = _split2(sp)
        su_lo = jnp.dot(jnp.concatenate([hi[:, :SBA_TK], lo[:, :SBA_TK]], axis=1), uu,
                        preferred_element_type=F32)
        su_hi = jnp.dot(jnp.concatenate([hi[:, SBA_TK:], lo[:, SBA_TK:]], axis=1), uu,
                        preferred_element_type=F32)
        carry = carry_ref[hh]
        later = jnp.concatenate([su_lo[:, :SBA_TK] + su_hi[:, SBA_TK:], su_hi[:, :SBA_TK]], axis=1)
        w = jnp.exp(z - sp - later - jnp.concatenate([carry, carry], axis=1))
        if masked:
            w = jnp.where(causal, w, 0.0)
        acc_ref[hh] += jnp.dot(w.astype(BF16), vblk, preferred_element_type=F32)
        carry_ref[hh] = carry + su_lo[:, SBA_TK:] + su_hi[:, SBA_TK:]


def _sba_kernel(q_ref, k_ref, v_ref, y_ref, carry_ref, acc_ref, *, seq):
    r = lax.broadcasted_iota(I32, (SBA_TK2, SBA_TK2), 0) % SBA_TK
    c = lax.broadcasted_iota(I32, (SBA_TK2, SBA_TK2), 1)
    uu = jnp.where((c >= SBA_TK) | (r > c), 1.0, 0.0).astype(BF16)
    lane = lax.broadcasted_iota(I32, (SBA_TQ, LANES), 1)
    assert SBA_TQ == SBA_TK2

    def qblock(qi, carry):
        row0 = pl.multiple_of(qi * SBA_TQ, SBA_TQ)
        qf = q_ref[pl.ds(row0, SBA_TQ), :] * jnp.asarray(SB_HEAD_DIM ** -0.5, BF16)
        zero = jnp.zeros_like(qf)
        qms = [jnp.where(lane < SB_HEAD_DIM, qf, zero), jnp.where(lane >= SB_HEAD_DIM, qf, zero)]
        carry_ref[...] = jnp.zeros_like(carry_ref)
        acc_ref[...] = jnp.zeros_like(acc_ref)
        _sba_step(qms, k_ref, v_ref, uu, carry_ref, acc_ref, qi, row0, True)

        def below(kk, c2):
            for d in range(2):
                _sba_step(qms, k_ref, v_ref, uu, carry_ref, acc_ref, qi - 1 - 2 * kk - d, row0, False)
            return c2

        lax.fori_loop(0, qi // 2, below, 0)

        @pl.when(qi % 2 == 1)
        def _():
            _sba_step(qms, k_ref, v_ref, uu, carry_ref, acc_ref, 0, row0, False)

        y = jnp.where(lane < SB_HEAD_DIM, acc_ref[0], acc_ref[1])
        y_ref[pl.ds(row0, SBA_TQ), :] = y.astype(BF16)
        return carry

    lax.fori_loop(0, seq // SBA_TQ, qblock, 0)


def _sba(sb_qkv, batch, seq):
    npair = SB_WIDTH // LANES
    blk = lambda off: pl.BlockSpec((None, seq, LANES), lambda b, p: (b, 0, off + p))
    return pl.pallas_call(
        functools.partial(_sba_kernel, seq=seq),
        grid=(batch, npair),
        in_specs=[blk(0), blk(npair), blk(2 * npair)],
        out_specs=pl.BlockSpec((None, seq, LANES), lambda b, p: (b, 0, p)),
        out_shape=jax.ShapeDtypeStruct((batch, seq, SB_WIDTH), BF16),
        scratch_shapes=[pltpu.VMEM((2, SBA_TQ, SBA_TK), F32), pltpu.VMEM((2, SBA_TQ, LANES), F32)],
        compiler_params=_params(),
        name="sba",
    )(sb_qkv, sb_qkv, sb_qkv)


def _conv_silu(x, w, b, row):
    y = x * w[CONV_WIDTH - 1:CONV_WIDTH, :] + b
    for k in range(1, CONV_WIDTH):
        shifted = jnp.where(row >= k, pltpu.roll(x, k, axis=0), 0.0)
        y = y + shifted * w[CONV_WIDTH - 1 - k:CONV_WIDTH - k, :]
    return y * jax.nn.sigmoid(y)


def _log_sigmoid(x):
    return -(jnp.maximum(-x, 0.0) + jnp.log1p(jnp.exp(-jnp.abs(x))))


def _mlstm_kernel(qk_ref, v_ref, o_ref, if_ref, cw_ref, cb_ref, ifb_ref, nw_ref, y_ref,
                  qs_ref, ks_ref, c_ref, n_ref, m_ref, *, seq):
    L = ML_CHUNK
    row = lax.broadcasted_iota(I32, (seq, LANES), 0)

    def conv_head(hd, carry):
        cq = pl.multiple_of(hd * LANES, LANES)
        ck = pl.multiple_of(ML_WIDTH + hd * LANES, LANES)
        q = _conv_silu(qk_ref[:, pl.ds(cq, LANES)], cw_ref[:, pl.ds(cq, LANES)], cb_ref[:, pl.ds(cq, LANES)], row)
        k = _conv_silu(qk_ref[:, pl.ds(ck, LANES)], cw_ref[:, pl.ds(ck, LANES)], cb_ref[:, pl.ds(ck, LANES)], row)
        qs_ref[hd] = q.astype(BF16)
        ks_ref[hd] = (k * (ML_HEAD_DIM ** -0.5)).astype(BF16)
        return carry

    lax.fori_loop(0, ML_HEADS, conv_head, 0)

    src = lax.broadcasted_iota(I32, (LANES, LANES), 0)
    rr = lax.broadcasted_iota(I32, (L, L), 0)
    cc = lax.broadcasted_iota(I32, (L, L), 1)
    tril = cc <= rr
    tril_b = jnp.where(tril, 1.0, 0.0).astype(BF16)
    eye = jnp.where(cc == rr, 1.0, 0.0)
    ones_b = jnp.ones((L, L), BF16)

    c_ref[...] = jnp.zeros_like(c_ref)
    n_ref[...] = jnp.zeros_like(n_ref)
    m_ref[...] = jnp.zeros_like(m_ref)

    def chunk(ci, carry):
        r0 = pl.multiple_of(ci * L, L)
        gates = if_ref[pl.ds(r0, L), :] + ifb_ref[...]
        log_f = _log_sigmoid(gates)
        for hd in range(ML_HEADS):
            lanes = slice(hd * LANES, (hd + 1) * LANES)
            ic = _dot_exact_lhs(gates, jnp.where(src == hd, 1.0, 0.0).astype(BF16))
            lf = _dot_exact_lhs(log_f, jnp.where(src == hd + ML_HEADS, 1.0, 0.0).astype(BF16))
            qc = qs_ref[hd, pl.ds(r0, L), :]
            kc = ks_ref[hd, pl.ds(r0, L), :]
            vc = v_ref[pl.ds(r0, L), lanes]
            m = m_ref[hd]
            n = n_ref[hd]
            cmat = c_ref[hd]
            b = _dot_exact_rhs(tril_b, lf)
            drow = _dot_exact_rhs(ones_b, eye * (ic - b))
            dmat = jnp.where(tril, b + drow, -jnp.inf)
            inter = b + m
            m_t = jnp.maximum(inter, jnp.max(dmat, axis=-1, keepdims=True))
            w_intra = jnp.exp(dmat - m_t)
            w_inter = jnp.exp(inter - m_t)
            s = lax.dot_general(qc, kc, (((1,), (1,)), ((), ())), preferred_element_type=F32) * w_intra
            num = jnp.dot(s.astype(BF16), vc, preferred_element_type=F32)
            num = num + w_inter * jnp.dot(qc, cmat.astype(BF16), preferred_element_type=F32)
            qn = jnp.sum(qc.astype(F32) * n, axis=-1, keepdims=True)
            den = jnp.sum(s, axis=-1, keepdims=True) + w_inter * qn
            h = num / jnp.maximum(jnp.abs(den), jnp.exp(-m_t))
            hn = h * lax.rsqrt(jnp.mean(h * h, axis=-1, keepdims=True) + RMS_EPS) * nw_ref[:, lanes]
            y_ref[pl.ds(r0, L), lanes] = (jax.nn.sigmoid(o_ref[pl.ds(r0, L), lanes]) * hn).astype(BF16)
            b_last = b[L - 1:L, :]
            g = b_last - b + ic
            m_new = jnp.maximum(b_last + m, jnp.max(g, axis=0, keepdims=True))
            decay = jnp.exp(b_last + m - m_new)
            kw = kc.astype(F32) * jnp.exp(g - m_new)
            c_ref[hd] = decay * cmat + lax.dot_general(
                kw.astype(BF16), vc, (((0,), (0,)), ((), ())), preferred_element_type=F32)
            n_ref[hd] = decay * n + jnp.sum(kw, axis=0, keepdims=True)
            m_ref[hd] = m_new
        return carry

    lax.fori_loop(0, seq // L, chunk, 0)


def _mlstm(ml_qk, ml_v, ml_o, ml_if, conv_w, conv_b, ifb, nw, batch, seq):
    rows = lambda n: pl.BlockSpec((None, seq, n), lambda b: (b, 0, 0))
    full = lambda a: pl.BlockSpec(a.shape, lambda b: (0, 0))
    return pl.pallas_call(
        functools.partial(_mlstm_kernel, seq=seq),
        grid=(batch,),
        in_specs=[rows(2 * ML_WIDTH), rows(ML_WIDTH), rows(ML_WIDTH), rows(LANES),
                  full(conv_w), full(conv_b), full(ifb), full(nw)],
        out_specs=rows(ML_WIDTH),
        out_shape=jax.ShapeDtypeStruct((batch, seq, ML_WIDTH), BF16),
        scratch_shapes=[pltpu.VMEM((ML_HEADS, seq, LANES), BF16), pltpu.VMEM((ML_HEADS, seq, LANES), BF16),
                        pltpu.VMEM((ML_HEADS, LANES, LANES), F32),
                        pltpu.VMEM((ML_HEADS, 1, LANES), F32), pltpu.VMEM((ML_HEADS, 1, LANES), F32)],
        compiler_params=_params(),
        name="mlstm",
    )(ml_qk, ml_v, ml_o, ml_if, conv_w, conv_b, ifb, nw)


def _to_slabs(a):
    t = a.shape[0]
    parts = [a[:, LANES * r:LANES * (r + 1)] for r in range(SUBLANES)]
    return jnp.stack(parts, axis=1).reshape(t * SUBLANES, LANES)


def _from_slabs(a):
    t = a.shape[0] // SUBLANES
    y = a.reshape(t, SUBLANES, LANES)
    return jnp.concatenate([y[:, r, :] for r in range(SUBLANES)], axis=1)


def _merge_kernel(ysb, yml, g_ref, x_ref, wsb, wml, wout, nw_ref, wq, k1_ref, k2_ref,
                  h_ref, hn_ref, off_ref, gate_ref, v1, i1, v2, i2, tv, te, gall, *, tm):
    a = jnp.dot(ysb[...], wsb[...], preferred_element_type=F32)
    b = jnp.dot(yml[...], wml[...], preferred_element_type=F32)
    g = g_ref[...]
    merged = jax.nn.sigmoid(g[:, :D_MODEL]) * a + jax.nn.sigmoid(g[:, D_MODEL:]) * b
    h = x_ref[...] + jnp.dot(merged.astype(BF16), wout[...], preferred_element_type=F32)
    h_ref[...] = _to_slabs(h)
    hn = h * lax.rsqrt(jnp.mean(h * h, axis=-1, keepdims=True) + RMS_EPS) * nw_ref[...]
    hn_ref[...] = _to_slabs(hn)
    q = jnp.dot(hn.astype(BF16), wq[...], preferred_element_type=F32).astype(BF16)
    _topk_kernel(q, k1_ref, k2_ref, off_ref, gate_ref, v1, i1, v2, i2, tv, te, gall, tt=tm)


def _merge(ysb, yml, gates, xt, wsb, wml, wout, nw, wq, k1, k2):
    t = xt.shape[0]
    tm = min(TM_PROJ, t)
    full = lambda a: pl.BlockSpec(a.shape, lambda i: (0, 0))
    tok = lambda n: pl.BlockSpec((tm, n), lambda i: (i, 0))
    slab = pl.BlockSpec((tm * SUBLANES, LANES), lambda i: (i, 0))
    kspec = pl.BlockSpec(k1.shape, lambda i: (0, 0, 0))
    small = lambda: pltpu.VMEM((PEER_TOPK, tm), F32)
    return pl.pallas_call(
        functools.partial(_merge_kernel, tm=tm),
        grid=(t // tm,),
        in_specs=[tok(SB_WIDTH), tok(ML_WIDTH), tok(2 * D_MODEL), tok(D_MODEL),
                  full(wsb), full(wml), full(wout), full(nw), full(wq), kspec, kspec],
        out_specs=[slab, slab, pl.BlockSpec((N_SEL, tm), lambda i: (0, i)), tok(N_SEL)],
        out_shape=[jax.ShapeDtypeStruct((t * SUBLANES, LANES), F32),
                   jax.ShapeDtypeStruct((t * SUBLANES, LANES), F32),
                   jax.ShapeDtypeStruct((N_SEL, t), I32),
                   jax.ShapeDtypeStruct((t, N_SEL), F32)],
        scratch_shapes=[small(), small(), small(), small(), small(), small(),
                        pltpu.VMEM((N_SEL, tm), F32)],
        compiler_params=_params(),
        name="merge_route",
    )(ysb, yml, gates, xt, wsb, wml, wout, nw, wq, k1, k2)


_BIG = 3.0e38


def _extract_top(vals, payload, val_ref, pay_ref):
    for it in range(PEER_TOPK):
        m = jnp.max(vals, axis=0, keepdims=True)
        p = jnp.min(jnp.where(vals == m, payload, _BIG), axis=0, keepdims=True)
        val_ref[it:it + 1, :] = m
        pay_ref[it:it + 1, :] = p
        vals = jnp.where(payload == p, -jnp.inf, vals)


def _batcher_pairs(n):
    pairs = []
    p = 1
    while p < n:
        k = p
        while k >= 1:
            for j in range(k % p, n - k, 2 * k):
                for i in range(min(k, n - j - k)):
                    if (i + j) // (2 * p) == (i + j + k) // (2 * p):
                        pairs.append((i + j, i + j + k))
            k //= 2
        p *= 2
    return pairs


def _extract_top_keys(s, val_ref, pay_ref):
    nv = PEER_N_KEYS // SUBLANES
    sub = lax.broadcasted_iota(I32, (SUBLANES, s.shape[1]), 0).astype(F32)
    vals = [s[SUBLANES * k:SUBLANES * (k + 1), :] for k in range(nv)]
    idxs = [sub + float(SUBLANES * k) for k in range(nv)]
    for x, y in _batcher_pairs(nv):
        a, b, ia, ib = vals[x], vals[y], idxs[x], idxs[y]
        keep = a >= b
        vals[x], vals[y] = jnp.where(keep, a, b), jnp.where(keep, b, a)
        idxs[x], idxs[y] = jnp.where(keep, ia, ib), jnp.where(keep, ib, ia)
    for it in range(PEER_TOPK):
        m = jnp.max(vals[0], axis=0, keepdims=True)
        p = jnp.min(jnp.where(vals[0] == m, idxs[0], _BIG), axis=0, keepdims=True)
        val_ref[it:it + 1, :] = m
        pay_ref[it:it + 1, :] = p
        win = idxs[0] == p
        for k in range(PEER_TOPK - 1 - it):
            vals[k] = jnp.where(win, vals[k + 1], vals[k])
            idxs[k] = jnp.where(win, idxs[k + 1], idxs[k])


def _topk_kernel(q_ref, k1_ref, k2_ref, off_ref, gate_ref, v1, i1, v2, i2, tv, te, gall, *, tt):
    dn = (((1,), (1,)), ((), ()))
    a8 = lax.broadcasted_iota(I32, (SUBLANES, tt), 0)
    for h in range(PEER_HEADS):
        q1 = q_ref[:, 2 * h * PEER_HALF:(2 * h + 1) * PEER_HALF]
        q2 = q_ref[:, (2 * h + 1) * PEER_HALF:(2 * h + 2) * PEER_HALF]
        s1 = lax.dot_general(k1_ref[h], q1, dn, preferred_element_type=F32)
        s2 = lax.dot_general(k2_ref[h], q2, dn, preferred_element_type=F32)
        _extract_top_keys(s1, v1, i1)
        _extract_top_keys(s2, v2, i2)
        va, vb = v1[...], v2[...]
        ea, eb = i1[...] * float(PEER_N_KEYS), i2[...]
        lo8 = slice(0, SUBLANES)
        hi8 = slice(SUBLANES, 2 * SUBLANES)
        cv = [va[0:1] + vb, va[1:2] + vb[lo8], va[hi8] + vb[0:1]]
        ce = [ea[0:1] + eb, ea[1:2] + eb[lo8], ea[hi8] + eb[0:1]]
        for bcol, amax in ((0, 7), (1, 7), (2, 4), (3, 3), (4, 2)):
            ok = (a8 >= 2) & (a8 <= amax)
            cv.append(jnp.where(ok, va[lo8] + vb[bcol:bcol + 1], -jnp.inf))
            ce.append(ea[lo8] + eb[bcol:bcol + 1])
        _extract_top(jnp.concatenate(cv, axis=0), jnp.concatenate(ce, axis=0), tv, te)
        sc = tv[...]
        ex = jnp.exp(sc - jnp.max(sc, axis=0, keepdims=True))
        rows = slice(h * PEER_TOPK, (h + 1) * PEER_TOPK)
        gall[rows, :] = ex / jnp.sum(ex, axis=0, keepdims=True)
        off_ref[rows, :] = te[...].astype(I32) * WORD_ROWS
    gate_ref[...] = jnp.transpose(gall[...])


def _off_copy(off_hbm, buf, sems, half_idx, s):
    return pltpu.make_async_copy(off_hbm.at[:, pl.ds(half_idx * PEER_HALF_TOK, PEER_HALF_TOK)], buf, sems.at[s])


def _run_halves(off_hbm, tab_hbm, off_a, off_b, tab_vmem, sems, run_half):
    i = pl.program_id(0)
    n = pl.num_programs(0)

    @pl.when(i == 0)
    def _():
        cp = pltpu.make_async_copy(tab_hbm, tab_vmem, sems.at[2])
        cp.start()
        _off_copy(off_hbm, off_a, sems, 0, 0).start()
        _off_copy(off_hbm, off_b, sems, 1, 1).start()
        cp.wait()

    for s, buf in enumerate((off_a, off_b)):
        _off_copy(off_hbm, buf, sems, 2 * i + s, s).wait()
        run_half(buf, s * PEER_HALF_TOK)

        @pl.when(i + 1 < n)
        def _():
            _off_copy(off_hbm, buf, sems, 2 * (i + 1) + s, s).start()


def _gather_token(off_smem, tab_vmem, col):
    rows = []
    for j in range(N_SEL):
        off = pl.multiple_of(off_smem.at[j][col], WORD_ROWS)
        rows.append(tab_vmem[pl.ds(off, WORD_ROWS), :])
    return pltpu.bitcast(jnp.concatenate(rows, axis=0), BF16)


def _slab_mask():
    s = lax.broadcasted_iota(I32, (SUBLANES, GATHER_ROWS), 0)
    n = lax.broadcasted_iota(I32, (SUBLANES, GATHER_ROWS), 1)
    return (n & (SUBLANES - 1)) == s


def _fold_matrix(shape, row_axis):
    n = lax.broadcasted_iota(I32, shape, row_axis)
    j = lax.broadcasted_iota(I32, shape, 1 - row_axis)
    return jnp.where(n // EXPERT_ROWS == j, 1.0, 0.0).astype(BF16)


def _merge_halves(a, b, m, k):
    r = pltpu.roll(b, k, axis=0)
    c = jnp.where(m, a, r)
    w = jnp.where(m, r, a)
    return c + pltpu.roll(w, SUBLANES - k, axis=0)


def _sublane_sums(xs, sub):
    m4, m2, m1 = sub < 4, (sub & 2) == 0, (sub & 1) == 0
    s = [_merge_halves(xs[a], xs[a + 4], m4, 4) for a in (0, 2, 1, 3)]
    return _merge_halves(_merge_halves(s[0], s[1], m2, 2), _merge_halves(s[2], s[3], m2, 2), m1, 1)


def _peer_u_kernel(off_hbm, x_ref, gate_ref, tab_hbm, w_ref, off_a, off_b, tab_vmem, cs_ref, sems):
    subw = lax.broadcasted_iota(I32, (SUBLANES, GATHER_ROWS), 0)
    mask = _slab_mask()
    dn = (((1,), (1,)), ((), ()))

    def run_half(off_smem, row_base):
        def group(g, carry):
            for oct_ in range(PEER_GROUP // SUBLANES):
                parts = []
                for tt in range(SUBLANES):
                    col = g * PEER_GROUP + oct_ * SUBLANES + tt
                    m = _gather_token(off_smem, tab_vmem, col)
                    r8 = pl.multiple_of((row_base + col) * SUBLANES, SUBLANES)
                    xh, xl = _split2(x_ref[pl.ds(r8, SUBLANES), :])
                    p = lax.dot_general(jnp.concatenate([xh, xl], axis=0), m, dn, preferred_element_type=F32)
                    parts.append(jnp.where(mask, p[:SUBLANES] + p[SUBLANES:], 0.0))
                r0 = pl.multiple_of(row_base + g * PEER_GROUP + oct_ * SUBLANES, SUBLANES)
                cs_ref[pl.ds(r0, SUBLANES), :] = _sublane_sums(parts, subw)
            return carry

        lax.fori_loop(0, PEER_HALF_TOK // PEER_GROUP, group, 0)

    _run_halves(off_hbm, tab_hbm, off_a, off_b, tab_vmem, sems, run_half)
    acts = _dot_exact_lhs(cs_ref[...], _fold_matrix((GATHER_ROWS, N_SEL), 0))
    w_ref[...] = gate_ref[...] * (0.5 * acts * (1.0 + lax.erf(acts * (2.0 ** -0.5))))


def _peer_scratch(tb, *extra):
    return [pltpu.SMEM((N_SEL, PEER_HALF_TOK), I32), pltpu.SMEM((N_SEL, PEER_HALF_TOK), I32),
            pltpu.VMEM((TAB_ROWS, LANES), I32), pltpu.VMEM((tb, GATHER_ROWS), F32), *extra,
            pltpu.SemaphoreType.DMA((3,))]


def _peer_u(off, x8, gate, tab):
    t = gate.shape[0]
    tb = TB_PEER
    tok = pl.BlockSpec((tb, N_SEL), lambda i: (i, 0))
    return pl.pallas_call(
        _peer_u_kernel,
        grid=(t // tb,),
        in_specs=[pl.BlockSpec(memory_space=pl.ANY),
                  pl.BlockSpec((tb * SUBLANES, LANES), lambda i: (i, 0)),
                  tok, pl.BlockSpec(memory_space=pl.ANY)],
        out_specs=tok,
        out_shape=jax.ShapeDtypeStruct((t, N_SEL), F32),
        scratch_shapes=_peer_scratch(tb),
        compiler_params=_params(),
        name="peer_u",
    )(off, x8, gate, tab)


def _peer_v_kernel(off_hbm, w_ref, h_ref, nw_ref, tab_hbm, out_ref, off_a, off_b, tab_vmem, wrep_ref, y_ref, sems):
    mask = _slab_mask()
    wrep_ref[...] = _dot_exact_lhs(w_ref[...], _fold_matrix((N_SEL, GATHER_ROWS), 1))

    def run_half(off_smem, row_base):
        def group(g, carry):
            for oct_ in range(PEER_GROUP // SUBLANES):
                r0 = pl.multiple_of(row_base + g * PEER_GROUP + oct_ * SUBLANES, SUBLANES)
                wrep = wrep_ref[pl.ds(r0, SUBLANES), :]
                for tt in range(SUBLANES):
                    col = g * PEER_GROUP + oct_ * SUBLANES + tt
                    m = _gather_token(off_smem, tab_vmem, col)
                    ah, al = _split2(jnp.where(mask, wrep[tt:tt + 1, :], 0.0))
                    res = jnp.dot(jnp.concatenate([ah, al], axis=0), m, preferred_element_type=F32)
                    r8 = pl.multiple_of((row_base + col) * SUBLANES, SUBLANES)
                    y = h_ref[pl.ds(r8, SUBLANES), :] + res[:SUBLANES] + res[SUBLANES:]
                    ss = jnp.sum(jnp.sum(y * y, axis=1, keepdims=True), axis=0, keepdims=True)
                    y_ref[pl.ds(r8, SUBLANES), :] = (
                        y * lax.rsqrt(ss * (1.0 / D_MODEL) + RMS_EPS) * nw_ref[...])
            return carry

        lax.fori_loop(0, PEER_HALF_TOK // PEER_GROUP, group, 0)

    _run_halves(off_hbm, tab_hbm, off_a, off_b, tab_vmem, sems, run_half)
    out_ref[...] = _from_slabs(y_ref[...])


def _peer_v(off, w, h8, nw8, tab):
    t = w.shape[0]
    tb = TB_PEER
    tok = pl.BlockSpec((tb, N_SEL), lambda i: (i, 0))
    return pl.pallas_call(
        _peer_v_kernel,
        grid=(t // tb,),
        in_specs=[pl.BlockSpec(memory_space=pl.ANY), tok,
                  pl.BlockSpec((tb * SUBLANES, LANES), lambda i: (i, 0)),
                  pl.BlockSpec((SUBLANES, LANES), lambda i: (0, 0)),
                  pl.BlockSpec(memory_space=pl.ANY)],
        out_specs=pl.BlockSpec((tb, D_MODEL), lambda i: (i, 0)),
        out_shape=jax.ShapeDtypeStruct((t, D_MODEL), F32),
        scratch_shapes=_peer_scratch(tb, pltpu.VMEM((tb * SUBLANES, LANES), F32)),
        compiler_params=_params(),
        name="peer_v",
    )(off, w, h8, nw8, tab)


def _table_kernel(w_ref, o_ref):
    o_ref[...] = pltpu.bitcast(_to_slabs(w_ref[...]).astype(BF16), I32)


def _expert_table(w):
    n = w.shape[0]
    eb = TABLE_BLOCK
    return pl.pallas_call(
        _table_kernel,
        grid=(n // eb,),
        in_specs=[pl.BlockSpec((eb, D_MODEL), lambda i: (i, 0))],
        out_specs=pl.BlockSpec((eb * WORD_ROWS, LANES), lambda i: (i, 0)),
        out_shape=jax.ShapeDtypeStruct((n * WORD_ROWS, LANES), I32),
        compiler_params=_params(),
        name="expert_table",
    )(w)


def kernel(x, norm_mix_w, w_in, conv_w, conv_b, b_igate, b_fgate, mlstm_norm_w, w_branch_sb, w_branch_ml, w_out, norm_ffn_w, peer_w_query, peer_keys1, peer_keys2, peer_u, peer_v, norm_final_w):
    batch, seq, d = x.shape
    t = batch * seq
    assert d == D_MODEL and w_in.shape[0] == 1
    assert seq % SBA_TQ == 0 and seq % ML_CHUNK == 0 and t % TM_PROJ == 0 and t % TT_TOPK == 0 and t % TB_PEER == 0
    xt = x.reshape(t, d)

    o_sb = 3 * SB_WIDTH
    o_qk = o_sb + 2 * ML_WIDTH
    o_v = o_qk + ML_WIDTH
    o_o = o_v + ML_WIDTH
    o_if = o_o + 2 * ML_HEADS
    w = w_in[0]
    wb = lambda a: a.astype(BF16)
    w_if = jnp.pad(w[:, o_o:o_if], ((0, 0), (0, LANES - 2 * ML_HEADS)))
    sb_qkv, ml_qk, ml_v, ml_o, gates, ml_if = _in_proj(
        xt, norm_mix_w[0].reshape(1, d), wb(w[:, :o_sb]), wb(w[:, o_sb:o_qk]), wb(w[:, o_qk:o_v]),
        wb(w[:, o_v:o_o]), wb(w[:, o_if:]), wb(w_if))

    y_sb = _sba(sb_qkv.reshape(batch, seq, 3 * SB_WIDTH), batch, seq)

    ifb = jnp.pad(jnp.concatenate([b_igate[0], b_fgate[0]]), (0, LANES - 2 * ML_HEADS)).reshape(1, LANES)
    y_ml = _mlstm(ml_qk.reshape(batch, seq, 2 * ML_WIDTH), ml_v.reshape(batch, seq, ML_WIDTH),
                  ml_o.reshape(batch, seq, ML_WIDTH), ml_if.reshape(batch, seq, LANES),
                  conv_w[0], conv_b[0].reshape(1, 2 * ML_WIDTH), ifb.astype(F32),
                  mlstm_norm_w[0].reshape(1, ML_WIDTH), batch, seq)

    h, hn, off, gate = _merge(y_sb.reshape(t, SB_WIDTH), y_ml.reshape(t, ML_WIDTH), gates, xt,
                              wb(w_branch_sb[0]), wb(w_branch_ml[0]), wb(w_out[0]),
                              norm_ffn_w[0].reshape(1, d), wb(peer_w_query[0]),
                              wb(peer_keys1[0]), wb(peer_keys2[0]))

    wsel = _peer_u(off, hn, gate, _expert_table(peer_u[0]))

    nw8 = norm_final_w.reshape(SUBLANES, LANES)
    out = _peer_v(off, wsel, h, nw8, _expert_table(peer_v[0]))
    return out.reshape(batch, seq, d)
```
